```python
import jax, jax.numpy as jnp
from jax import lax
import numpy as np

D_MODEL = 1024
BATCH = 16
SEQ = 4096
DEPTH = 2
DEC_BATCH = 2
DEC_SEQ = 16384
PAST_LEN = 128

HEAD_DIM = 64
EPS = 1e-6
NEG = -1e30
A_HEADS = 8
A_KV_HEADS = 2
A_GROUP = A_HEADS // A_KV_HEADS
A_WINDOW = 128
B_PATTERNS = ((128, 1), (512, 4), (2048, 16))
B_HEADS_PER_GROUP = 4
B_HEADS = B_HEADS_PER_GROUP * len(B_PATTERNS)
C_HEADS = 16
C_KV_HEADS = 4
C_GROUP = C_HEADS // C_KV_HEADS
C_BLOCK = 128
ROPE_THETA = 10000.0
GRID_W = 64
D_FF = ((8 * D_MODEL + 3 * 256 - 1) // (3 * 256)) * 256
AB_IN = (A_HEADS + 2 * A_KV_HEADS) * HEAD_DIM + 3 * B_HEADS * HEAD_DIM
AB_OUT = (A_HEADS + B_HEADS_PER_GROUP) * HEAD_DIM
C_IN = (C_HEADS + 2 * C_KV_HEADS) * HEAD_DIM
C_OUT = C_HEADS * HEAD_DIM
N_EVEN = (DEPTH + 1) // 2
N_ODD = DEPTH // 2

kernel_name = "hybrid_window_dilated_axialrope_encoder"


def rmsnorm(x, g):
    xf = x.astype(jnp.float32)
    y = xf * lax.rsqrt(jnp.mean(xf * xf, axis=-1, keepdims=True) + EPS)
    return (y * g.astype(jnp.float32)).astype(x.dtype)


def alibi_slopes(n):
    return jnp.asarray(2.0 ** (-8.0 * np.arange(1, n + 1) / n), dtype=jnp.float32)


def banded_attention(q, k, v, radius, dist_scale, slopes, sink=None):
    n, L, KV, G, dh = q.shape
    blk = radius
    nb = -(-L // blk)
    Lp = nb * blk
    qb = jnp.pad(q, ((0, 0), (0, Lp - L), (0, 0), (0, 0), (0, 0))).reshape(n, nb, blk, KV, G, dh)
    kv_pad = ((0, 0), (blk, Lp - L + blk), (0, 0), (0, 0))
    kp = jnp.pad(k, kv_pad).reshape(n, nb + 2, blk, KV, dh)
    vp = jnp.pad(v, kv_pad).reshape(n, nb + 2, blk, KV, dh)
    kw = jnp.concatenate([kp[:, :-2], kp[:, 1:-1], kp[:, 2:]], axis=2)
    vw = jnp.concatenate([vp[:, :-2], vp[:, 1:-1], vp[:, 2:]], axis=2)
    s = jnp.einsum('nbqhgd,nbkhd->nbhgqk', qb, kw, preferred_element_type=jnp.float32) * (dh ** -0.5)
    rel = jnp.arange(3 * blk)[None, :] - blk - jnp.arange(blk)[:, None]
    key_pos = jnp.arange(nb)[:, None] * blk - blk + jnp.arange(3 * blk)[None, :]
    valid = (jnp.abs(rel) <= radius)[None] & ((key_pos >= 0) & (key_pos < L))[:, None, :]
    bias = -(slopes.astype(jnp.float32) * dist_scale)[:, :, None, None] * jnp.abs(rel).astype(jnp.float32)
    s = jnp.where(valid[None, :, None, None], s + bias, NEG)
    m = jnp.max(s, axis=-1)
    if sink is not None:
        sink_f = sink.astype(jnp.float32)[:, :, None]
        m = jnp.maximum(m, sink_f)
    p = jnp.exp(s - m[..., None])
    den = jnp.sum(p, axis=-1)
    if sink is not None:
        den = den + jnp.exp(sink_f - m)
    o = jnp.einsum('nbhgqk,nbkhd->nbqhgd', p.astype(v.dtype), vw, preferred_element_type=jnp.float32)
    o = o / jnp.moveaxis(den, -1, 2)[..., None]
    lse = jnp.moveaxis(m + jnp.log(den), -1, 2)
    return o.reshape(n, Lp, KV, G, dh)[:, :L], lse.reshape(n, Lp, KV, G)[:, :L]


def dilated_attention(q, k, v, window, dil, slopes):
    n, T, _ = q.shape
    H = B_HEADS_PER_GROUP
    Ls = T // dil

    def by_stride(x):
        return x.reshape(n, Ls, dil, H, HEAD_DIM).transpose(0, 2, 1, 3, 4).reshape(n * dil, Ls, H, HEAD_DIM)

    o, lse = banded_attention(by_stride(q)[:, :, :, None], by_stride(k), by_stride(v),
                              window // (2 * dil), dil, slopes)
    o = o.reshape(n, dil, Ls, H, HEAD_DIM).transpose(0, 2, 1, 3, 4).reshape(n, T, H, HEAD_DIM)
    lse = lse.reshape(n, dil, Ls, H).transpose(0, 2, 1, 3).reshape(n, T, H)
    return o, lse


def mixer_ab(h, w_in, w_out, sink):
    n, T, _ = h.shape
    proj = h @ w_in
    dq, dkv, db = A_HEADS * HEAD_DIM, A_KV_HEADS * HEAD_DIM, B_HEADS_PER_GROUP * HEAD_DIM
    qa = proj[..., :dq].reshape(n, T, A_KV_HEADS, A_GROUP, HEAD_DIM)
    ka = proj[..., dq:dq + dkv].reshape(n, T, A_KV_HEADS, HEAD_DIM)
    va = proj[..., dq + dkv:dq + 2 * dkv].reshape(n, T, A_KV_HEADS, HEAD_DIM)
    oa, _ = banded_attention(qa, ka, va, A_WINDOW, 1,
                             alibi_slopes(A_HEADS).reshape(A_KV_HEADS, A_GROUP),
                             sink.reshape(A_KV_HEADS, A_GROUP))
    oa = oa.reshape(n, T, dq).astype(h.dtype)
    slopes_b = alibi_slopes(B_HEADS).reshape(len(B_PATTERNS), B_HEADS_PER_GROUP, 1)
    base0 = dq + 2 * dkv
    outs, lses = [], []
    for i, (window, dil) in enumerate(B_PATTERNS):
        base = base0 + i * 3 * db
        o, lse = dilated_attention(proj[..., base:base + db], proj[..., base + db:base + 2 * db],
                                   proj[..., base + 2 * db:base + 3 * db], window, dil, slopes_b[i])
        outs.append(o)
        lses.append(lse)
    wts = jax.nn.softmax(jnp.stack(lses, axis=0), axis=0)
    ob = jnp.sum(wts[..., None] * jnp.stack(outs, axis=0), axis=0).reshape(n, T, db).astype(h.dtype)
    return jnp.concatenate([oa, ob], axis=-1) @ w_out


def rope_2d(x, rows):
    n_freq = HEAD_DIM // 4
    inv = ROPE_THETA ** (-jnp.arange(n_freq, dtype=jnp.float32) / n_freq)
    row = jnp.repeat(jnp.arange(rows, dtype=jnp.float32), GRID_W)
    col = jnp.tile(jnp.arange(GRID_W, dtype=jnp.float32), rows)
    ang = jnp.concatenate([row[:, None] * inv, col[:, None] * inv], axis=-1)
    c, s = jnp.cos(ang)[:, None, :], jnp.sin(ang)[:, None, :]
    xr = x.astype(jnp.float32).reshape(x.shape[:-1] + (HEAD_DIM // 2, 2))
    x0, x1 = xr[..., 0], xr[..., 1]
    out = jnp.stack([x0 * c - x1 * s, x0 * s + x1 * c], axis=-1)
    return out.reshape(x.shape).astype(x.dtype)


def mixer_c(h, w_in, w_out, q_gain, k_gain):
    n, T, _ = h.shape
    rows = T // GRID_W
    proj = h @ w_in
    dq, dkv = C_HEADS * HEAD_DIM, C_KV_HEADS * HEAD_DIM
    q = proj[..., :dq].reshape(n, T, C_HEADS, HEAD_DIM)
    k = proj[..., dq:dq + dkv].reshape(n, T, C_KV_HEADS, HEAD_DIM)
    v = proj[..., dq + dkv:].reshape(n, T, C_KV_HEADS, HEAD_DIM)
    q = rope_2d(rmsnorm(q, q_gain), rows)
    k = rope_2d(rmsnorm(k, k_gain), rows)
    nb = T // C_BLOCK
    qb = q.reshape(n, nb, C_BLOCK, C_KV_HEADS, C_GROUP, HEAD_DIM).transpose(1, 0, 2, 3, 4, 5)
    scale = HEAD_DIM ** -0.5

    def attend_block(qblk):
        s = jnp.einsum('nqhgd,nkhd->nhgqk', qblk, k, preferred_element_type=jnp.float32) * scale
        p = jax.nn.softmax(s, axis=-1)
        return jnp.einsum('nhgqk,nkhd->nqhgd', p.astype(v.dtype), v,
                          preferred_element_type=jnp.float32).astype(v.dtype)

    o = lax.map(attend_block, qb)
    o = o.transpose(1, 0, 2, 3, 4, 5).reshape(n, T, C_OUT)
    return o @ w_out


def swiglu(h, w1, w3, w2):
    return (jax.nn.silu(h @ w1) * (h @ w3)) @ w2


def trunk(x, norm_mix, w_in_ab, w_out_ab, sink_a, w_in_c, w_out_c, q_gain_c, k_gain_c,
          norm_ffn, ffn_w1, ffn_w3, ffn_w2, final_norm):
    for layer in range(DEPTH):
        h = rmsnorm(x, norm_mix[layer])
        i = layer // 2
        if layer % 2 == 0:
            h = mixer_ab(h, w_in_ab[i], w_out_ab[i], sink_a[i])
        else:
            h = mixer_c(h, w_in_c[i], w_out_c[i], q_gain_c[i], k_gain_c[i])
        x = x + h
        x = x + swiglu(rmsnorm(x, norm_ffn[layer]), ffn_w1[layer], ffn_w3[layer], ffn_w2[layer])
    return rmsnorm(x, final_norm)


def setup_inputs(seed: int = 0) -> dict:
    key = jax.random.key(seed)
    ks = jax.random.split(key, 16)
    f32 = jnp.float32
    nrm = lambda k, shape, scale: jax.random.normal(k, shape, f32) * scale
    return {
        "x_prompt": nrm(ks[0], (BATCH, SEQ, D_MODEL), 1.0),
        "x_sample": nrm(ks[1], (DEC_BATCH, DEC_SEQ, D_MODEL), 1.0),
        "norm_mix": 1.0 + nrm(ks[2], (DEPTH, D_MODEL), 0.05),
        "w_in_ab": nrm(ks[3], (N_EVEN, D_MODEL, AB_IN), D_MODEL ** -0.5),
        "w_out_ab": nrm(ks[4], (N_EVEN, AB_OUT, D_MODEL), AB_OUT ** -0.5),
        "sink_a": nrm(ks[5], (N_EVEN, A_HEADS), 1.0),
        "w_in_c": nrm(ks[6], (N_ODD, D_MODEL, C_IN), D_MODEL ** -0.5),
        "w_out_c": nrm(ks[7], (N_ODD, C_OUT, D_MODEL), C_OUT ** -0.5),
        "q_gain_c": 1.0 + nrm(ks[8], (N_ODD, HEAD_DIM), 0.05),
        "k_gain_c": 1.0 + nrm(ks[9], (N_ODD, HEAD_DIM), 0.05),
        "norm_ffn": 1.0 + nrm(ks[10], (DEPTH, D_MODEL), 0.05),
        "ffn_w1": nrm(ks[11], (DEPTH, D_MODEL, D_FF), D_MODEL ** -0.5),
        "ffn_w3": nrm(ks[12], (DEPTH, D_MODEL, D_FF), D_MODEL ** -0.5),
        "ffn_w2": nrm(ks[13], (DEPTH, D_FF, D_MODEL), D_FF ** -0.5),
        "final_norm": 1.0 + nrm(ks[14], (D_MODEL,), 0.05),
    }


def reference(x_prompt, x_sample, norm_mix, w_in_ab, w_out_ab, sink_a, w_in_c, w_out_c,
              q_gain_c, k_gain_c, norm_ffn, ffn_w1, ffn_w3, ffn_w2, final_norm):
    y_prompt = trunk(x_prompt, norm_mix, w_in_ab, w_out_ab, sink_a, w_in_c, w_out_c, q_gain_c,
                     k_gain_c, norm_ffn, ffn_w1, ffn_w3, ffn_w2, final_norm)
    y_sample = trunk(x_sample, norm_mix, w_in_ab, w_out_ab, sink_a, w_in_c, w_out_c, q_gain_c,
                     k_gain_c, norm_ffn, ffn_w1, ffn_w3, ffn_w2, final_norm)
    return (y_prompt, y_sample)
```

```python
import functools

import numpy as np
import jax
import jax.numpy as jnp
from jax import lax
from jax.experimental import pallas as pl
from jax.experimental.pallas import tpu as pltpu

F32 = jnp.float32
BF16 = jnp.bfloat16

D_MODEL = 1024
HEAD_DIM = 64
EPS = 1e-6
NEG = -1e30
A_HEADS = 8
A_KV_HEADS = 2
A_WINDOW = 128
B_PATTERNS = ((128, 1), (512, 4), (2048, 16))
B_HEADS_PER_GROUP = 4
B_HEADS = B_HEADS_PER_GROUP * len(B_PATTERNS)
B_RADIUS = 64
B_WIDTH = B_HEADS_PER_GROUP * HEAD_DIM
C_HEADS = 16
C_KV_HEADS = 4
C_GROUP = C_HEADS // C_KV_HEADS
ROPE_THETA = 10000.0
GRID_W = 64
D_FF = 2816
A_Q = A_HEADS * HEAD_DIM
A_KV = A_KV_HEADS * HEAD_DIM
PA_WIDTH = A_Q + 2 * A_KV + 3 * B_WIDTH
GROUP_WIDTH = 3 * B_WIDTH
AB_OUT = A_Q + B_WIDTH
C_Q = C_HEADS * HEAD_DIM
C_KV = C_KV_HEADS * HEAD_DIM
C_IN = C_Q + 2 * C_KV
QK_SCALE = HEAD_DIM ** -0.5

LANES = 128
ROW_TILE = 512
BAND_TQ = 256
FLASH_TQ = 256
FLASH_TK = ROW_TILE
FF_CHUNKS = ((0, 1536), (1536, D_FF))
VMEM_LIMIT = 56 * 1024 * 1024


def _alibi_slopes(n):
    return np.asarray(2.0 ** (-8.0 * np.arange(1, n + 1) / n), dtype=np.float32)


def _const_spec(shape):
    zeros = (0,) * len(shape)
    return pl.BlockSpec(shape, lambda *_: zeros, pipeline_mode=pl.Buffered(1))


def _params(sem):
    return pltpu.CompilerParams(dimension_semantics=sem, vmem_limit_bytes=VMEM_LIMIT)


def _rms(x, g):
    return x * lax.rsqrt(jnp.mean(x * x, axis=-1, keepdims=True) + EPS) * g


def _ffn(x1, g_ref, w1_ref, w3_ref, w2_ref):
    h = _rms(x1, g_ref[...]).astype(BF16)
    acc = None
    for c0, c1 in FF_CHUNKS:
        u = jnp.dot(h, w1_ref[:, c0:c1], preferred_element_type=F32)
        v = jnp.dot(h, w3_ref[:, c0:c1], preferred_element_type=F32)
        t = (u * (1.0 / (1.0 + jnp.exp(-u))) * v).astype(BF16)
        part = jnp.dot(t, w2_ref[c0:c1, :], preferred_element_type=F32)
        acc = part if acc is None else acc + part
    return x1 + acc


def _proj_ab_kernel(x_ref, g_ref, w_ref, pa_ref, pb1_ref, pb2_ref, slab_ref, *, tm):
    h = _rms(x_ref[...], g_ref[...]).astype(BF16)
    pa_ref[...] = jnp.dot(h, w_ref[:, :PA_WIDTH], preferred_element_type=F32).astype(BF16)
    n_slab = GROUP_WIDTH // LANES
    for (_, dil), out_ref, c0 in ((B_PATTERNS[1], pb1_ref, PA_WIDTH),
                                  (B_PATTERNS[2], pb2_ref, PA_WIDTH + GROUP_WIDTH)):
        r = jnp.dot(h, w_ref[:, c0:c0 + GROUP_WIDTH], preferred_element_type=F32)
        for s in range(n_slab):
            slab_ref[s] = r[:, s * LANES:(s + 1) * LANES]
        rows = tm // dil
        for d in range(dil):
            for s in range(n_slab):
                out_ref[d, :, s * LANES:(s + 1) * LANES] = (
                    slab_ref[s, pl.ds(d, rows, stride=dil), :].astype(BF16))


def _proj_ab(x, g, w):
    n, T, _ = x.shape
    tm = ROW_TILE
    d1, d2 = B_PATTERNS[1][1], B_PATTERNS[2][1]
    return pl.pallas_call(
        functools.partial(_proj_ab_kernel, tm=tm),
        grid=(n, T // tm),
        in_specs=[
            pl.BlockSpec((None, tm, D_MODEL), lambda b, i: (b, i, 0)),
            _const_spec((1, D_MODEL)),
            _const_spec(w.shape),
        ],
        out_specs=[
            pl.BlockSpec((None, tm, PA_WIDTH), lambda b, i: (b, i, 0)),
            pl.BlockSpec((None, d1, tm // d1, GROUP_WIDTH), lambda b, i: (b, 0, i, 0)),
            pl.BlockSpec((None, d2, tm // d2, GROUP_WIDTH), lambda b, i: (b, 0, i, 0)),
        ],
        out_shape=[
            jax.ShapeDtypeStruct((n, T, PA_WIDTH), BF16),
            jax.ShapeDtypeStruct((n, d1, T // d1, GROUP_WIDTH), BF16),
            jax.ShapeDtypeStruct((n, d2, T // d2, GROUP_WIDTH), BF16),
        ],
        scratch_shapes=[pltpu.VMEM((GROUP_WIDTH // LANES, tm, LANES), F32)],
        compiler_params=_params(("parallel", "parallel")),
        name="proj_ab",
    )(x, g, w)


def _band_kernel(*refs, tq, radius, seq, n_q, n_kv, slopes, has_sink, emit_lse):
    q_ref, kp_ref, kc_ref, kn_ref, vp_ref, vc_ref, vn_ref = refs[:7]
    pos = 7
    sink_ref = None
    if has_sink:
        sink_ref = refs[pos]
        pos += 1
    o_ref = refs[pos]
    lse_ref = refs[pos + 1] if emit_lse else None

    i = pl.program_id(1)
    width = tq + 2 * radius
    row = lax.broadcasted_iota(jnp.int32, (tq, width), 0)
    col = lax.broadcasted_iota(jnp.int32, (tq, width), 1)
    dist = jnp.abs(col - radius - row)
    kpos = i * tq - radius + col
    valid = (dist <= radius) & (kpos >= 0) & (kpos < seq)
    distf = dist.astype(F32)

    k_all = jnp.concatenate([kp_ref[...], kc_ref[...], kn_ref[...]], axis=0)
    v_all = jnp.concatenate([vp_ref[...], vc_ref[...], vn_ref[...]], axis=0)
    group = n_q // n_kv
    outs, lses = [], []
    for j in range(n_kv):
        kj = k_all[:, j * HEAD_DIM:(j + 1) * HEAD_DIM]
        vj = v_all[:, j * HEAD_DIM:(j + 1) * HEAD_DIM]
        for g in range(group):
            h = j * group + g
            qh = q_ref[:, h * HEAD_DIM:(h + 1) * HEAD_DIM]
            s = lax.dot_general(qh, kj, (((1,), (1,)), ((), ())), preferred_element_type=F32)
            s = jnp.where(valid, s - slopes[h] * distf, NEG)
            m = jnp.max(s, axis=-1, keepdims=True)
            if has_sink:
                sink = sink_ref[0, h]
                m = jnp.maximum(m, sink)
            p = jnp.exp(s - m)
            den = jnp.sum(p, axis=-1, keepdims=True)
            if has_sink:
                den = den + jnp.exp(sink - m)
            o = jnp.dot(p.astype(BF16), vj, preferred_element_type=F32) / den
            outs.append(o)
            if emit_lse:
                lses.append(jnp.broadcast_to(m + jnp.log(den), (tq, HEAD_DIM)))
    o_ref[...] = jnp.concatenate(outs, axis=-1).astype(o_ref.dtype)
    if emit_lse:
        lse_ref[...] = jnp.concatenate(lses, axis=-1)


def _band_attention(src, *, q_col, k_col, v_col, q_width, kv_width, radius, slopes,
                    sink=None, emit_lse):
    nb, seq, _ = src.shape
    tq = min(BAND_TQ, seq)
    rb = tq // radius
    last = seq // radius - 1
    n_q = q_width // HEAD_DIM
    n_kv = kv_width // HEAD_DIM

    def cur(col):
        return lambda b, i: (b, i, col)

    def prev(col):
        return lambda b, i: (b, jnp.maximum(i * rb - 1, 0), col)

    def nxt(col):
        return lambda b, i: (b, jnp.minimum((i + 1) * rb, last), col)

    in_specs = [pl.BlockSpec((None, tq, q_width), cur(q_col))]
    for col in (k_col, v_col):
        in_specs += [
            pl.BlockSpec((None, radius, kv_width), prev(col)),
            pl.BlockSpec((None, tq, kv_width), cur(col)),
            pl.BlockSpec((None, radius, kv_width), nxt(col)),
        ]
    args = [src] * 7
    if sink is not None:
        in_specs.append(pl.BlockSpec(memory_space=pltpu.SMEM))
        args.append(sink)
    out_dtype = F32 if emit_lse else BF16
    out_specs = [pl.BlockSpec((None, tq, q_width), lambda b, i: (b, i, 0))]
    out_shape = [jax.ShapeDtypeStruct((nb, seq, q_width), out_dtype)]
    if emit_lse:
        out_specs.append(pl.BlockSpec((None, tq, q_width), lambda b, i: (b, i, 0)))
        out_shape.append(jax.ShapeDtypeStruct((nb, seq, q_width), F32))
    kernel = functools.partial(
        _band_kernel, tq=tq, radius=radius, seq=seq, n_q=n_q, n_kv=n_kv,
        slopes=tuple(float(s) for s in slopes), has_sink=sink is not None, emit_lse=emit_lse)
    return pl.pallas_call(
        kernel,
        grid=(nb, seq // tq),
        in_specs=in_specs,
        out_specs=out_specs,
        out_shape=out_shape,
        compiler_params=_params(("parallel", "parallel")),
        name="band_attention",
    )(*args)


def _mid_kernel(x_ref, oa_ref, o0_ref, l0_ref, o1_ref, l1_ref, o2_ref, l2_ref, wo_ref,
                g_ref, w1_ref, w3_ref, w2_ref, out_ref, slab_ref, *, tm):
    n_slab = B_WIDTH // LANES
    for base, src, (_, dil) in ((0, o1_ref, B_PATTERNS[1]), (n_slab, l1_ref, B_PATTERNS[1]),
                                (2 * n_slab, o2_ref, B_PATTERNS[2]),
                                (3 * n_slab, l2_ref, B_PATTERNS[2])):
        rows = tm // dil
        for d in range(dil):
            for s in range(n_slab):
                slab_ref[base + s, pl.ds(d, rows, stride=dil), :] = (
                    src[d, :, s * LANES:(s + 1) * LANES])
    mixed = []
    for s in range(n_slab):
        sl = slice(s * LANES, (s + 1) * LANES)
        l0, l1, l2 = l0_ref[:, sl], slab_ref[n_slab + s], slab_ref[3 * n_slab + s]
        mx = jnp.maximum(jnp.maximum(l0, l1), l2)
        e0, e1, e2 = jnp.exp(l0 - mx), jnp.exp(l1 - mx), jnp.exp(l2 - mx)
        num = e0 * o0_ref[:, sl] + e1 * slab_ref[s] + e2 * slab_ref[2 * n_slab + s]
        mixed.append(num / (e0 + e1 + e2))
    ob = jnp.concatenate(mixed, axis=-1).astype(BF16)
    attn = (jnp.dot(oa_ref[...], wo_ref[:A_Q, :], preferred_element_type=F32)
            + jnp.dot(ob, wo_ref[A_Q:, :], preferred_element_type=F32))
    out_ref[...] = _ffn(x_ref[...] + attn, g_ref, w1_ref, w3_ref, w2_ref)


def _mid(x, oa, o0, l0, o1, l1, o2, l2, wo, g, w1, w3, w2):
    n, T, _ = x.shape
    tm = ROW_TILE
    d1, d2 = B_PATTERNS[1][1], B_PATTERNS[2][1]
    tok = lambda width: pl.BlockSpec((None, tm, width), lambda b, i: (b, i, 0))
    res = lambda dil: pl.BlockSpec((None, dil, tm // dil, B_WIDTH), lambda b, i: (b, 0, i, 0))
    return pl.pallas_call(
        functools.partial(_mid_kernel, tm=tm),
        grid=(n, T // tm),
        in_specs=[tok(D_MODEL), tok(A_Q), tok(B_WIDTH), tok(B_WIDTH), res(d1), res(d1),
                  res(d2), res(d2), _const_spec(wo.shape), _const_spec((1, D_MODEL)),
                  _const_spec(w1.shape), _const_spec(w3.shape), _const_spec(w2.shape)],
        out_specs=tok(D_MODEL),
        out_shape=jax.ShapeDtypeStruct((n, T, D_MODEL), F32),
        scratch_shapes=[pltpu.VMEM((4 * B_WIDTH // LANES, tm, LANES), F32)],
        compiler_params=_params(("parallel", "parallel")),
        name="mix_out_ffn",
    )(x, oa, o0, l0, o1, l1, o2, l2, wo, g, w1, w3, w2)


def _norm_rope(xh, gain, cos, sin):
    half = HEAD_DIM // 2
    y = xh * lax.rsqrt(jnp.mean(xh * xh, axis=0, keepdims=True) + EPS) * gain
    x0, x1 = y[:half], y[half:]
    return jnp.concatenate([x0 * cos - x1 * sin, x0 * sin + x1 * cos], axis=0)


def _proj_c_kernel(x_ref, g_ref, w_ref, gq_ref, gk_ref, cq_ref, sq_ref, ck_ref, sk_ref,
                   qt_ref, k_ref, vt_ref):
    h = _rms(x_ref[...], g_ref[...]).astype(BF16)
    pt = lax.dot_general(w_ref[...], h, (((1,), (1,)), ((), ())), preferred_element_type=F32)
    gq, gk = gq_ref[...], gk_ref[...]
    cq, sq, ck, sk = cq_ref[...], sq_ref[...], ck_ref[...], sk_ref[...]
    for hd in range(C_HEADS):
        rows = slice(hd * HEAD_DIM, (hd + 1) * HEAD_DIM)
        qt_ref[rows, :] = _norm_rope(pt[rows], gq, cq, sq).astype(BF16)
    kt = jnp.concatenate(
        [_norm_rope(pt[C_Q + j * HEAD_DIM:C_Q + (j + 1) * HEAD_DIM], gk, ck, sk)
         for j in range(C_KV_HEADS)], axis=0)
    k_nat = kt.T
    for j in range(C_KV_HEADS):
        k_ref[j] = k_nat[:, j * HEAD_DIM:(j + 1) * HEAD_DIM].astype(BF16)
    vt_ref[...] = pt[C_Q + C_KV:].astype(BF16)


def _proj_c(x, g, wt, gq, gk, cq, sq, ck, sk):
    n, T, _ = x.shape
    tm = ROW_TILE
    half = HEAD_DIM // 2
    tab = pl.BlockSpec((half, tm), lambda b, i: (0, i))
    return pl.pallas_call(
        _proj_c_kernel,
        grid=(n, T // tm),
        in_specs=[
            pl.BlockSpec((None, tm, D_MODEL), lambda b, i: (b, i, 0)),
            _const_spec((1, D_MODEL)),
            _const_spec(wt.shape),
            _const_spec((HEAD_DIM, tm)),
            _const_spec((HEAD_DIM, tm)),
            tab, tab, tab, tab,
        ],
        out_specs=[
            pl.BlockSpec((None, C_Q, tm), lambda b, i: (b, 0, i)),
            pl.BlockSpec((None, C_KV_HEADS, tm, HEAD_DIM), lambda b, i: (b, 0, i, 0)),
            pl.BlockSpec((None, None, C_KV, tm), lambda b, i: (b, i, 0, 0)),
        ],
        out_shape=[
            jax.ShapeDtypeStruct((n, C_Q, T), BF16),
            jax.ShapeDtypeStruct((n, C_KV_HEADS, T, HEAD_DIM), BF16),
            jax.ShapeDtypeStruct((n, T // tm, C_KV, tm), BF16),
        ],
        compiler_params=_params(("parallel", "parallel")),
        name="proj_c",
    )(x, g, wt, gq, gk, cq, sq, ck, sk)


def _flash_kernel(qt_ref, k_ref, vt_ref, o_ref, acc_ref, *, tq, tk, n_chunks):
    acc_ref[...] = jnp.zeros_like(acc_ref)

    def body(j, carry):
        ms, ls = carry
        kb = k_ref[pl.ds(pl.multiple_of(j * tk, tk), tk), :]
        vb = vt_ref[j]
        new_ms, new_ls = [], []
        for hh in range(C_GROUP):
            rows = slice(hh * HEAD_DIM, (hh + 1) * HEAD_DIM)
            s = jnp.dot(kb, qt_ref[rows, :], preferred_element_type=F32)
            m_new = jnp.maximum(ms[hh], jnp.max(s, axis=0, keepdims=True))
            alpha = jnp.exp(ms[hh] - m_new)
            p = jnp.exp(s - m_new)
            new_ls.append(alpha * ls[hh] + jnp.sum(p, axis=0, keepdims=True))
            new_ms.append(m_new)
            pv = jnp.dot(vb, p.astype(BF16), preferred_element_type=F32)
            acc_ref[rows, :] = alpha * acc_ref[rows, :] + pv
        return tuple(new_ms), tuple(new_ls)

    init = (tuple(jnp.full((1, tq), NEG, F32) for _ in range(C_GROUP)),
            tuple(jnp.zeros((1, tq), F32) for _ in range(C_GROUP)))
    _, ls = lax.fori_loop(0, n_chunks, body, init)
    ot = jnp.concatenate(
        [acc_ref[hh * HEAD_DIM:(hh + 1) * HEAD_DIM, :] / ls[hh] for hh in range(C_GROUP)], axis=0)
    o_ref[...] = ot.T.astype(BF16)


def _flash(qt, k, vt):
    n, _, T = qt.shape
    tq, tk = FLASH_TQ, FLASH_TK
    n_chunks = T // tk
    width = C_GROUP * HEAD_DIM
    return pl.pallas_call(
        functools.partial(_flash_kernel, tq=tq, tk=tk, n_chunks=n_chunks),
        grid=(n, C_KV_HEADS, T // tq),
        in_specs=[
            pl.BlockSpec((None, width, tq), lambda b, g, i: (b, g, i)),
            pl.BlockSpec((None, None, T, HEAD_DIM), lambda b, g, i: (b, g, 0, 0)),
            pl.BlockSpec((None, n_chunks, HEAD_DIM, tk), lambda b, g, i: (b, 0, g, 0)),
        ],
        out_specs=pl.BlockSpec((None, tq, width), lambda b, g, i: (b, i, g)),
        out_shape=jax.ShapeDtypeStruct((n, T, C_Q), BF16),
        scratch_shapes=[pltpu.VMEM((width, tq), F32)],
        compiler_params=_params(("parallel", "parallel", "parallel")),
        name="flash_c",
    )(qt, k, vt)


def _out_kernel(x_ref, o_ref, wo_ref, g_ref, w1_ref, w3_ref, w2_ref, gf_ref, out_ref):
    x1 = x_ref[...] + jnp.dot(o_ref[...], wo_ref[...], preferred_element_type=F32)
    out_ref[...] = _rms(_ffn(x1, g_ref, w1_ref, w3_ref, w2_ref), gf_ref[...])


def _out(x, o, wo, g, w1, w3, w2, gf):
    n, T, _ = x.shape
    tm = ROW_TILE
    tok = pl.BlockSpec((None, tm, D_MODEL), lambda b, i: (b, i, 0))
    return pl.pallas_call(
        _out_kernel,
        grid=(n, T // tm),
        in_specs=[tok, tok, _const_spec(wo.shape), _const_spec((1, D_MODEL)),
                  _const_spec(w1.shape), _const_spec(w3.shape), _const_spec(w2.shape),
                  _const_spec((1, D_MODEL))],
        out_specs=tok,
        out_shape=jax.ShapeDtypeStruct((n, T, D_MODEL), F32),
        compiler_params=_params(("parallel", "parallel")),
        name="out_ffn_norm",
    )(x, o, wo, g, w1, w3, w2, gf)


def _rope_tables(T):
    n_freq = HEAD_DIM // 4
    inv = ROPE_THETA ** (-jnp.arange(n_freq, dtype=F32) / n_freq)
    rows = T // GRID_W
    row = jnp.repeat(jnp.arange(rows, dtype=F32), GRID_W)
    col = jnp.tile(jnp.arange(GRID_W, dtype=F32), rows)
    ang = jnp.concatenate([row[:, None] * inv, col[:, None] * inv], axis=-1)
    return jnp.cos(ang).T, jnp.sin(ang).T


def _prepare(norm_mix, w_in_ab, w_out_ab, sink_a, w_in_c, w_out_c, q_gain_c, k_gain_c,
             norm_ffn, ffn_w1, ffn_w3, ffn_w2, final_norm):
    col_scale = np.ones((1, PA_WIDTH + 2 * GROUP_WIDTH), np.float32)
    col_scale[:, :A_Q] = QK_SCALE
    for gidx in range(len(B_PATTERNS)):
        base = A_Q + 2 * A_KV + gidx * GROUP_WIDTH
        col_scale[:, base:base + B_WIDTH] = QK_SCALE
    pair_perm = np.concatenate([np.arange(0, HEAD_DIM, 2), np.arange(1, HEAD_DIM, 2)])
    row_perm = np.concatenate(
        [h * HEAD_DIM + pair_perm for h in range(C_HEADS + C_KV_HEADS)]
        + [np.arange(C_Q + C_KV, C_IN)])
    row = lambda v: v.reshape(1, D_MODEL).astype(F32)
    gain = lambda v: jnp.broadcast_to(v.astype(F32)[pair_perm][:, None], (HEAD_DIM, ROW_TILE))
    return dict(
        g_mix0=row(norm_mix[0]), g_mix1=row(norm_mix[1]),
        g_ffn0=row(norm_ffn[0]), g_ffn1=row(norm_ffn[1]), g_final=row(final_norm),
        w_ab=(w_in_ab[0] * col_scale).astype(BF16),
        wo_ab=w_out_ab[0].astype(BF16),
        sink=sink_a[0].reshape(1, A_HEADS).astype(F32),
        wt_c=w_in_c[0].T[row_perm].astype(BF16),
        wo_c=w_out_c[0].astype(BF16),
        gq=gain(q_gain_c[0]), gk=gain(k_gain_c[0]),
        w1=[ffn_w1[l].astype(BF16) for l in range(2)],
        w3=[ffn_w3[l].astype(BF16) for l in range(2)],
        w2=[ffn_w2[l].astype(BF16) for l in range(2)],
    )


def _trunk(x, p):
    n, T, _ = x.shape
    pa, pb1, pb2 = _proj_ab(x, p["g_mix0"], p["w_ab"])
    slopes_a = _alibi_slopes(A_HEADS)
    (oa,) = _band_attention(
        pa, q_col=0, k_col=A_Q // A_KV, v_col=A_Q // A_KV + 1, q_width=A_Q, kv_width=A_KV,
        radius=A_WINDOW, slopes=slopes_a, sink=p["sink"], emit_lse=False)
    slopes_b = _alibi_slopes(B_HEADS).reshape(len(B_PATTERNS), B_HEADS_PER_GROUP)
    ob, lb = [], []
    for gidx, (src, first) in enumerate((
            (pa, (A_Q + 2 * A_KV) // B_WIDTH),
            (pb1.reshape(n * B_PATTERNS[1][1], T // B_PATTERNS[1][1], GROUP_WIDTH), 0),
            (pb2.reshape(n * B_PATTERNS[2][1], T // B_PATTERNS[2][1], GROUP_WIDTH), 0))):
        dil = B_PATTERNS[gidx][1]
        o, lse = _band_attention(
            src, q_col=first, k_col=first + 1, v_col=first + 2, q_width=B_WIDTH,
            kv_width=B_WIDTH, radius=B_RADIUS, slopes=slopes_b[gidx] * np.float32(dil),
            emit_lse=True)
        ob.append(o.reshape(n, dil, T // dil, B_WIDTH) if dil > 1 else o)
        lb.append(lse.reshape(n, dil, T // dil, B_WIDTH) if dil > 1 else lse)
    x = _mid(x, oa, ob[0], lb[0], ob[1], lb[1], ob[2], lb[2], p["wo_ab"], p["g_ffn0"],
             p["w1"][0], p["w3"][0], p["w2"][0])
    cos, sin = _rope_tables(T)
    qt, k, vt = _proj_c(x, p["g_mix1"], p["wt_c"], p["gq"], p["gk"],
                        cos * QK_SCALE, sin * QK_SCALE, cos, sin)
    o = _flash(qt, k, vt)
    return _out(x, o, p["wo_c"], p["g_ffn1"], p["w1"][1], p["w3"][1], p["w2"][1], p["g_final"])


def kernel(x_prompt, x_sample, norm_mix, w_in_ab, w_out_ab, sink_a, w_in_c, w_out_c, q_gain_c,
           k_gain_c, norm_ffn, ffn_w1, ffn_w3, ffn_w2, final_norm):
    assert norm_mix.shape[0] == 2 and all(window // (2 * dil) == B_RADIUS
                                          for window, dil in B_PATTERNS)
    p = _prepare(norm_mix, w_in_ab, w_out_ab, sink_a, w_in_c, w_out_c, q_gain_c, k_gain_c,
                 norm_ffn, ffn_w1, ffn_w3, ffn_w2, final_norm)
    return _trunk(x_prompt, p), _trunk(x_sample, p)
```

```python
import functools

import numpy as np
import jax
import jax.numpy as jnp
from jax import lax
from jax.experimental import pallas as pl
from jax.experimental.pallas import tpu as pltpu

F32 = jnp.float32
BF16 = jnp.bfloat16

D_MODEL = 1024
HEAD_DIM = 64
EPS = 1e-6
NEG = -1e30
A_HEADS = 8
A_KV_HEADS = 2
A_WINDOW = 128
B_PATTERNS = ((128, 1), (512, 4), (2048, 16))
B_HEADS_PER_GROUP = 4
B_HEADS = B_HEADS_PER_GROUP * len(B_PATTERNS)
B_RADIUS = 64
B_WIDTH = B_HEADS_PER_GROUP * HEAD_DIM
C_HEADS = 16
C_KV_HEADS = 4
C_GROUP = C_HEADS // C_KV_HEADS
ROPE_THETA = 10000.0
GRID_W = 64
D_FF = 2816
A_Q = A_HEADS * HEAD_DIM
A_KV = A_KV_HEADS * HEAD_DIM
PA_WIDTH = A_Q + 2 * A_KV + 3 * B_WIDTH
GROUP_WIDTH = 3 * B_WIDTH
AB_OUT = A_Q + B_WIDTH
C_Q = C_HEADS * HEAD_DIM
C_KV = C_KV_HEADS * HEAD_DIM
C_IN = C_Q + 2 * C_KV
QK_SCALE = HEAD_DIM ** -0.5
LOG2E = 1.4426950408889634
V_ROWS = HEAD_DIM + 16

LANES = 128
ROW_TILE = 512
BAND_TQ = 256
FLASH_TQ = 256
FLASH_TK = ROW_TILE
FF_CHUNKS = ((0, 1536), (1536, D_FF))
VMEM_LIMIT = 56 * 1024 * 1024


def _alibi_slopes(n):
    return np.asarray(2.0 ** (-8.0 * np.arange(1, n + 1) / n), dtype=np.float32)


def _const_spec(shape):
    zeros = (0,) * len(shape)
    return pl.BlockSpec(shape, lambda *_: zeros, pipeline_mode=pl.Buffered(1))


def _params(sem):
    return pltpu.CompilerParams(dimension_semantics=sem, vmem_limit_bytes=VMEM_LIMIT)


def _rms(x, g):
    return x * lax.rsqrt(jnp.mean(x * x, axis=-1, keepdims=True) + EPS) * g


def _ffn(x1, g_ref, w1_ref, w3_ref, w2_ref):
    h = _rms(x1, g_ref[...]).astype(BF16)
    acc = None
    for c0, c1 in FF_CHUNKS:
        u = jnp.dot(h, w1_ref[:, c0:c1], preferred_element_type=F32)
        v = jnp.dot(h, w3_ref[:, c0:c1], preferred_element_type=F32)
        t = (u * (1.0 / (1.0 + jnp.exp(-u))) * v).astype(BF16)
        part = jnp.dot(t, w2_ref[c0:c1, :], preferred_element_type=F32)
        acc = part if acc is None else acc + part
    return x1 + acc


def _proj_ab_kernel(x_ref, g_ref, w_ref, pa_ref, pb1_ref, pb2_ref, slab_ref, *, tm):
    h = _rms(x_ref[...], g_ref[...]).astype(BF16)
    pa_ref[...] = jnp.dot(h, w_ref[:, :PA_WIDTH], preferred_element_type=F32).astype(BF16)
    n_slab = GROUP_WIDTH // LANES
    for (_, dil), out_ref, c0 in ((B_PATTERNS[1], pb1_ref, PA_WIDTH),
                                  (B_PATTERNS[2], pb2_ref, PA_WIDTH + GROUP_WIDTH)):
        r = jnp.dot(h, w_ref[:, c0:c0 + GROUP_WIDTH], preferred_element_type=F32)
        for s in range(n_slab):
            slab_ref[s] = r[:, s * LANES:(s + 1) * LANES]
        rows = tm // dil
        for d in range(dil):
            for s in range(n_slab):
                out_ref[d, :, s * LANES:(s + 1) * LANES] = (
                    slab_ref[s, pl.ds(d, rows, stride=dil), :].astype(BF16))


def _proj_ab(x, g, w):
    n, T, _ = x.shape
    tm = ROW_TILE
    d1, d2 = B_PATTERNS[1][1], B_PATTERNS[2][1]
    return pl.pallas_call(
        functools.partial(_proj_ab_kernel, tm=tm),
        grid=(n, T // tm),
        in_specs=[
            pl.BlockSpec((None, tm, D_MODEL), lambda b, i: (b, i, 0)),
            _const_spec((1, D_MODEL)),
            _const_spec(w.shape),
        ],
        out_specs=[
            pl.BlockSpec((None, tm, PA_WIDTH), lambda b, i: (b, i, 0)),
            pl.BlockSpec((None, d1, tm // d1, GROUP_WIDTH), lambda b, i: (b, 0, i, 0)),
            pl.BlockSpec((None, d2, tm // d2, GROUP_WIDTH), lambda b, i: (b, 0, i, 0)),
        ],
        out_shape=[
            jax.ShapeDtypeStruct((n, T, PA_WIDTH), BF16),
            jax.ShapeDtypeStruct((n, d1, T // d1, GROUP_WIDTH), BF16),
            jax.ShapeDtypeStruct((n, d2, T // d2, GROUP_WIDTH), BF16),
        ],
        scratch_shapes=[pltpu.VMEM((GROUP_WIDTH // LANES, tm, LANES), F32)],
        compiler_params=_params(("parallel", "parallel")),
        name="proj_ab",
    )(x, g, w)


def _band_kernel(*refs, n_q, n_kv, has_sink, emit_lse):
    q_ref, kp_ref, kc_ref, kn_ref, vp_ref, vc_ref, vn_ref, bias_ref = refs[:8]
    pos = 8
    sink_ref = None
    if has_sink:
        sink_ref = refs[pos]
        pos += 1
    o_ref = refs[pos]
    pos += 1
    lse_ref = None
    if emit_lse:
        lse_ref = refs[pos]
        pos += 1
    s_scr, m_scr = refs[pos], refs[pos + 1]

    k_all = jnp.concatenate([kp_ref[...], kc_ref[...], kn_ref[...]], axis=0)
    v_all = jnp.concatenate([vp_ref[...], vc_ref[...], vn_ref[...]], axis=0)
    ones = jnp.ones((v_all.shape[0], HEAD_DIM), BF16)
    group = n_q // n_kv
    ks = [k_all[:, j * HEAD_DIM:(j + 1) * HEAD_DIM] for j in range(n_kv)]
    vs = [jnp.concatenate([v_all[:, j * HEAD_DIM:(j + 1) * HEAD_DIM], ones], axis=-1)
          for j in range(n_kv)]
    sinks = [sink_ref[0, h] * LOG2E for h in range(n_q)] if has_sink else None

    def score_phase(h):
        qh = q_ref[:, h * HEAD_DIM:(h + 1) * HEAD_DIM]
        s = lax.dot_general(qh, ks[h // group], (((1,), (1,)), ((), ())),
                            preferred_element_type=F32) + bias_ref[h]
        m = jnp.max(s, axis=-1, keepdims=True)
        if has_sink:
            m = jnp.maximum(m, sinks[h])
        s_scr[h % 2] = s
        m_scr[h % 2] = m

    def value_phase(h):
        m = m_scr[h % 2]
        p = jnp.exp2(s_scr[h % 2] - m).astype(BF16)
        oe = jnp.dot(p, vs[h // group], preferred_element_type=F32)
        den = oe[:, HEAD_DIM:]
        if has_sink:
            den = den + jnp.exp2(sinks[h] - m)
        return oe[:, :HEAD_DIM] / den, m + jnp.log2(den)

    score_phase(0)
    pair_o, pair_l = [], []
    for h in range(n_q):
        if h + 1 < n_q:
            score_phase(h + 1)
        o, lse = value_phase(h)
        pair_o.append(o)
        pair_l.append(lse)
        if h % 2 == 1:
            lanes = slice((h - 1) * HEAD_DIM, (h + 1) * HEAD_DIM)
            o_ref[:, lanes] = jnp.concatenate(pair_o, axis=-1).astype(o_ref.dtype)
            if emit_lse:
                lse_ref[:, lanes] = jnp.concatenate(pair_l, axis=-1)
            pair_o, pair_l = [], []


def _band_bias(tq, radius, slopes):
    width = tq + 2 * radius
    row = lax.broadcasted_iota(jnp.int32, (tq, width), 0)
    col = lax.broadcasted_iota(jnp.int32, (tq, width), 1)
    dist = jnp.abs(col - radius - row)
    slope2 = jnp.asarray(np.asarray(slopes, np.float32) * np.float32(LOG2E))
    alibi = -slope2[:, None, None] * dist.astype(F32)[None]
    tables = []
    for cls in range(4):
        ok = dist <= radius
        if cls & 1:
            ok = ok & (col >= radius)
        if cls & 2:
            ok = ok & (col < radius + tq)
        tables.append(jnp.where(ok[None], alibi, NEG))
    return jnp.stack(tables, axis=0)


def _band_attention(src, *, q_col, k_col, v_col, q_width, kv_width, radius, slopes,
                    sink=None, emit_lse):
    nb, seq, _ = src.shape
    tq = min(BAND_TQ, seq)
    rb = tq // radius
    n_blk = seq // tq
    last = seq // radius - 1
    n_q = q_width // HEAD_DIM
    n_kv = kv_width // HEAD_DIM
    width = tq + 2 * radius

    def cur(col):
        return lambda i, b: (b, i, col)

    def prev(col):
        return lambda i, b: (b, jnp.maximum(i * rb - 1, 0), col)

    def nxt(col):
        return lambda i, b: (b, jnp.minimum((i + 1) * rb, last), col)

    def edge_class(i, b):
        cls = (i == 0).astype(jnp.int32) + 2 * (i == n_blk - 1).astype(jnp.int32)
        return (cls, 0, 0, 0)

    in_specs = [pl.BlockSpec((None, tq, q_width), cur(q_col))]
    for col in (k_col, v_col):
        in_specs += [
            pl.BlockSpec((None, radius, kv_width), prev(col)),
            pl.BlockSpec((None, tq, kv_width), cur(col)),
            pl.BlockSpec((None, radius, kv_width), nxt(col)),
        ]
    in_specs.append(pl.BlockSpec((None, n_q, tq, width), edge_class))
    args = [src] * 7 + [_band_bias(tq, radius, slopes)]
    if sink is not None:
        in_specs.append(pl.BlockSpec(memory_space=pltpu.SMEM))
        args.append(sink)
    out_dtype = F32 if emit_lse else BF16
    out_specs = [pl.BlockSpec((None, tq, q_width), lambda i, b: (b, i, 0))]
    out_shape = [jax.ShapeDtypeStruct((nb, seq, q_width), out_dtype)]
    if emit_lse:
        out_specs.append(pl.BlockSpec((None, tq, q_width), lambda i, b: (b, i, 0)))
        out_shape.append(jax.ShapeDtypeStruct((nb, seq, q_width), F32))
    kernel = functools.partial(
        _band_kernel, n_q=n_q, n_kv=n_kv, has_sink=sink is not None, emit_lse=emit_lse)
    return pl.pallas_call(
        kernel,
        grid=(n_blk, nb),
        in_specs=in_specs,
        out_specs=out_specs,
        out_shape=out_shape,
        scratch_shapes=[pltpu.VMEM((2, tq, width), F32), pltpu.VMEM((2, tq, 1), F32)],
        compiler_params=_params(("parallel", "parallel")),
        name="band_attention",
    )(*args)


def _mid_kernel(x_ref, oa_ref, o0_ref, l0_ref, o1_ref, l1_ref, o2_ref, l2_ref, wo_ref,
                g_ref, w1_ref, w3_ref, w2_ref, out_ref, slab_ref, *, tm):
    n_slab = B_WIDTH // LANES
    for base, src, (_, dil) in ((0, o1_ref, B_PATTERNS[1]), (n_slab, l1_ref, B_PATTERNS[1]),
                                (2 * n_slab, o2_ref, B_PATTERNS[2]),
                                (3 * n_slab, l2_ref, B_PATTERNS[2])):
        rows = tm // dil
        for d in range(dil):
            for s in range(n_slab):
                slab_ref[base + s, pl.ds(d, rows, stride=dil), :] = (
                    src[d, :, s * LANES:(s + 1) * LANES])
    mixed = []
    for s in range(n_slab):
        sl = slice(s * LANES, (s + 1) * LANES)
        l0, l1, l2 = l0_ref[:, sl], slab_ref[n_slab + s], slab_ref[3 * n_slab + s]
        mx = jnp.maximum(jnp.maximum(l0, l1), l2)
        e0, e1, e2 = jnp.exp2(l0 - mx), jnp.exp2(l1 - mx), jnp.exp2(l2 - mx)
        num = e0 * o0_ref[:, sl] + e1 * slab_ref[s] + e2 * slab_ref[2 * n_slab + s]
        mixed.append(num / (e0 + e1 + e2))
    ob = jnp.concatenate(mixed, axis=-1).astype(BF16)
    attn = (jnp.dot(oa_ref[...], wo_ref[:A_Q, :], preferred_element_type=F32)
            + jnp.dot(ob, wo_ref[A_Q:, :], preferred_element_type=F32))
    out_ref[...] = _ffn(x_ref[...] + attn, g_ref, w1_ref, w3_ref, w2_ref)


def _mid(x, oa, o0, l0, o1, l1, o2, l2, wo, g, w1, w3, w2):
    n, T, _ = x.shape
    tm = ROW_TILE
    d1, d2 = B_PATTERNS[1][1], B_PATTERNS[2][1]
    tok = lambda width: pl.BlockSpec((None, tm, width), lambda b, i: (b, i, 0))
    res = lambda dil: pl.BlockSpec((None, dil, tm // dil, B_WIDTH), lambda b, i: (b, 0, i, 0))
    return pl.pallas_call(
        functools.partial(_mid_kernel, tm=tm),
        grid=(n, T // tm),
        in_specs=[tok(D_MODEL), tok(A_Q), tok(B_WIDTH), tok(B_WIDTH), res(d1), res(d1),
                  res(d2), res(d2), _const_spec(wo.shape), _const_spec((1, D_MODEL)),
                  _const_spec(w1.shape), _const_spec(w3.shape), _const_spec(w2.shape)],
        out_specs=tok(D_MODEL),
        out_shape=jax.ShapeDtypeStruct((n, T, D_MODEL), F32),
        scratch_shapes=[pltpu.VMEM((4 * B_WIDTH // LANES, tm, LANES), F32)],
        compiler_params=_params(("parallel", "parallel")),
        name="mix_out_ffn",
    )(x, oa, o0, l0, o1, l1, o2, l2, wo, g, w1, w3, w2)


def _norm_rope(xh, gain, cos, sin):
    half = HEAD_DIM // 2
    y = xh * lax.rsqrt(jnp.mean(xh * xh, axis=0, keepdims=True) + EPS) * gain
    x0, x1 = y[:half], y[half:]
    return jnp.concatenate([x0 * cos - x1 * sin, x0 * sin + x1 * cos], axis=0)


def _proj_c_kernel(x_ref, g_ref, w_ref, gq_ref, gk_ref, cq_ref, sq_ref, ck_ref, sk_ref,
                   qt_ref, k_ref, vt_ref):
    h = _rms(x_ref[...], g_ref[...]).astype(BF16)
    pt = lax.dot_general(w_ref[...], h, (((1,), (1,)), ((), ())), preferred_element_type=F32)
    gq, gk = gq_ref[...], gk_ref[...]
    cq, sq, ck, sk = cq_ref[...], sq_ref[...], ck_ref[...], sk_ref[...]
    for hd in range(C_HEADS):
        rows = slice(hd * HEAD_DIM, (hd + 1) * HEAD_DIM)
        qt_ref[rows, :] = _norm_rope(pt[rows], gq, cq, sq).astype(BF16)
    kt = jnp.concatenate(
        [_norm_rope(pt[C_Q + j * HEAD_DIM:C_Q + (j + 1) * HEAD_DIM], gk, ck, sk)
         for j in range(C_KV_HEADS)], axis=0)
    k_nat = kt.T
    for j in range(C_KV_HEADS):
        k_ref[j] = k_nat[:, j * HEAD_DIM:(j + 1) * HEAD_DIM].astype(BF16)
    pad = V_ROWS - HEAD_DIM
    ones_row = (lax.broadcasted_iota(jnp.int32, (pad, pt.shape[1]), 0) == 0).astype(BF16)
    for j in range(C_KV_HEADS):
        v0 = C_Q + C_KV + j * HEAD_DIM
        vt_ref[j * V_ROWS:j * V_ROWS + HEAD_DIM, :] = pt[v0:v0 + HEAD_DIM].astype(BF16)
        vt_ref[j * V_ROWS + HEAD_DIM:(j + 1) * V_ROWS, :] = ones_row


def _proj_c(x, g, wt, gq, gk, cq, sq, ck, sk):
    n, T, _ = x.shape
    tm = ROW_TILE
    half = HEAD_DIM // 2
    tab = pl.BlockSpec((half, tm), lambda b, i: (0, i))
    return pl.pallas_call(
        _proj_c_kernel,
        grid=(n, T // tm),
        in_specs=[
            pl.BlockSpec((None, tm, D_MODEL), lambda b, i: (b, i, 0)),
            _const_spec((1, D_MODEL)),
            _const_spec(wt.shape),
            _const_spec((HEAD_DIM, tm)),
            _const_spec((HEAD_DIM, tm)),
            tab, tab, tab, tab,
        ],
        out_specs=[
            pl.BlockSpec((None, C_Q, tm), lambda b, i: (b, 0, i)),
            pl.BlockSpec((None, C_KV_HEADS, tm, HEAD_DIM), lambda b, i: (b, 0, i, 0)),
            pl.BlockSpec((None, None, C_KV_HEADS * V_ROWS, tm), lambda b, i: (b, i, 0, 0)),
        ],
        out_shape=[
            jax.ShapeDtypeStruct((n, C_Q, T), BF16),
            jax.ShapeDtypeStruct((n, C_KV_HEADS, T, HEAD_DIM), BF16),
            jax.ShapeDtypeStruct((n, T // tm, C_KV_HEADS * V_ROWS, tm), BF16),
        ],
        compiler_params=_params(("parallel", "parallel")),
        name="proj_c",
    )(x, g, wt, gq, gk, cq, sq, ck, sk)


def _flash_kernel(qt_ref, k_ref, vt_ref, o_ref, sa_ref, sb_ref, mxa_ref, mxb_ref, m_ref, acc_ref,
                  *, tq, tk, n_chunks):
    acc_ref[...] = jnp.zeros_like(acc_ref)
    m_ref[...] = jnp.full_like(m_ref, NEG)

    def scores(j, s_ref, mx_ref):
        kb = k_ref[pl.ds(pl.multiple_of(j * tk, tk), tk), :]
        for hh in range(C_GROUP):
            s = jnp.dot(kb, qt_ref[hh * HEAD_DIM:(hh + 1) * HEAD_DIM, :],
                        preferred_element_type=F32)
            s_ref[hh] = s
            mx_ref[hh] = jnp.max(s, axis=0, keepdims=True)

    def accumulate(j, s_ref, mx_ref):
        vb = vt_ref[j]
        for hh in range(C_GROUP):
            m_prev = m_ref[hh]
            m_new = jnp.maximum(m_prev, mx_ref[hh])
            alpha = jnp.exp2(m_prev - m_new)
            p = jnp.exp2(s_ref[hh] - m_new).astype(BF16)
            pv = jnp.dot(vb, p, preferred_element_type=F32)
            acc_ref[hh] = alpha * acc_ref[hh] + pv
            m_ref[hh] = m_new

    scores(0, sa_ref, mxa_ref)

    def pair(jj, carry):
        j = 2 * jj
        scores(j + 1, sb_ref, mxb_ref)
        accumulate(j, sa_ref, mxa_ref)
        scores(j + 2, sa_ref, mxa_ref)
        accumulate(j + 1, sb_ref, mxb_ref)
        return carry

    lax.fori_loop(0, n_chunks // 2 - 1, pair, 0)
    scores(n_chunks - 1, sb_ref, mxb_ref)
    accumulate(n_chunks - 2, sa_ref, mxa_ref)
    accumulate(n_chunks - 1, sb_ref, mxb_ref)
    ot = jnp.concatenate(
        [acc_ref[hh, :HEAD_DIM, :] / acc_ref[hh, HEAD_DIM:HEAD_DIM + 1, :]
         for hh in range(C_GROUP)], axis=0)
    o_ref[...] = ot.T.astype(BF16)


def _flash(qt, k, vt):
    n, _, T = qt.shape
    tq, tk = FLASH_TQ, FLASH_TK
    n_chunks = T // tk
    assert n_chunks % 2 == 0
    width = C_GROUP * HEAD_DIM
    return pl.pallas_call(
        functools.partial(_flash_kernel, tq=tq, tk=tk, n_chunks=n_chunks),
        grid=(n, C_KV_HEADS, T // tq),
        in_specs=[
            pl.BlockSpec((None, width, tq), lambda b, g, i: (b, g, i)),
            pl.BlockSpec((None, None, T, HEAD_DIM), lambda b, g, i: (b, g, 0, 0)),
            pl.BlockSpec((None, n_chunks, V_ROWS, tk), lambda b, g, i: (b, 0, g, 0)),
        ],
        out_specs=pl.BlockSpec((None, tq, width), lambda b, g, i: (b, i, g)),
        out_shape=jax.ShapeDtypeStruct((n, T, C_Q), BF16),
        scratch_shapes=[
            pltpu.VMEM((C_GROUP, tk, tq), F32), pltpu.VMEM((C_GROUP, tk, tq), F32),
            pltpu.VMEM((C_GROUP, 1, tq), F32), pltpu.VMEM((C_GROUP, 1, tq), F32),
            pltpu.VMEM((C_GROUP, 1, tq), F32), pltpu.VMEM((C_GROUP, V_ROWS, tq), F32),
        ],
        compiler_params=_params(("parallel", "parallel", "parallel")),
        name="flash_c",
    )(qt, k, vt)


def _out_kernel(x_ref, o_ref, wo_ref, g_ref, w1_ref, w3_ref, w2_ref, gf_ref, out_ref):
    x1 = x_ref[...] + jnp.dot(o_ref[...], wo_ref[...], preferred_element_type=F32)
    out_ref[...] = _rms(_ffn(x1, g_ref, w1_ref, w3_ref, w2_ref), gf_ref[...])


def _out(x, o, wo, g, w1, w3, w2, gf):
    n, T, _ = x.shape
    tm = ROW_TILE
    tok = pl.BlockSpec((None, tm, D_MODEL), lambda b, i: (b, i, 0))
    return pl.pallas_call(
        _out_kernel,
        grid=(n, T // tm),
        in_specs=[tok, tok, _const_spec(wo.shape), _const_spec((1, D_MODEL)),
                  _const_spec(w1.shape), _const_spec(w3.shape), _const_spec(w2.shape),
                  _const_spec((1, D_MODEL))],
        out_specs=tok,
        out_shape=jax.ShapeDtypeStruct((n, T, D_MODEL), F32),
        compiler_params=_params(("parallel", "parallel")),
        name="out_ffn_norm",
    )(x, o, wo, g, w1, w3, w2, gf)


def _rope_tables(T):
    n_freq = HEAD_DIM // 4
    inv = ROPE_THETA ** (-jnp.arange(n_freq, dtype=F32) / n_freq)
    rows = T // GRID_W
    row = jnp.repeat(jnp.arange(rows, dtype=F32), GRID_W)
    col = jnp.tile(jnp.arange(GRID_W, dtype=F32), rows)
    ang = jnp.concatenate([row[:, None] * inv, col[:, None] * inv], axis=-1)
    return jnp.cos(ang).T, jnp.sin(ang).T


def _prepare(norm_mix, w_in_ab, w_out_ab, sink_a, w_in_c, w_out_c, q_gain_c, k_gain_c,
             norm_ffn, ffn_w1, ffn_w3, ffn_w2, final_norm):
    col_scale = np.ones((1, PA_WIDTH + 2 * GROUP_WIDTH), np.float32)
    col_scale[:, :A_Q] = QK_SCALE * LOG2E
    for gidx in range(len(B_PATTERNS)):
        base = A_Q + 2 * A_KV + gidx * GROUP_WIDTH
        col_scale[:, base:base + B_WIDTH] = QK_SCALE * LOG2E
    pair_perm = np.concatenate([np.arange(0, HEAD_DIM, 2), np.arange(1, HEAD_DIM, 2)])
    row_perm = np.concatenate(
        [h * HEAD_DIM + pair_perm for h in range(C_HEADS + C_KV_HEADS)]
        + [np.arange(C_Q + C_KV, C_IN)])
    row = lambda v: v.reshape(1, D_MODEL).astype(F32)
    gain = lambda v: jnp.broadcast_to(v.astype(F32)[pair_perm][:, None], (HEAD_DIM, ROW_TILE))
    return dict(
        g_mix0=row(norm_mix[0]), g_mix1=row(norm_mix[1]),
        g_ffn0=row(norm_ffn[0]), g_ffn1=row(norm_ffn[1]), g_final=row(final_norm),
        w_ab=(w_in_ab[0] * col_scale).astype(BF16),
        wo_ab=w_out_ab[0].astype(BF16),
        sink=sink_a[0].reshape(1, A_HEADS).astype(F32),
        wt_c=w_in_c[0].T[row_perm].astype(BF16),
        wo_c=w_out_c[0].astype(BF16),
        gq=gain(q_gain_c[0]), gk=gain(k_gain_c[0]),
        w1=[ffn_w1[l].astype(BF16) for l in range(2)],
        w3=[ffn_w3[l].astype(BF16) for l in range(2)],
        w2=[ffn_w2[l].astype(BF16) for l in range(2)],
    )


def _trunk(x, p):
    n, T, _ = x.shape
    pa, pb1, pb2 = _proj_ab(x, p["g_mix0"], p["w_ab"])
    slopes_a = _alibi_slopes(A_HEADS)
    (oa,) = _band_attention(
        pa, q_col=0, k_col=A_Q // A_KV, v_col=A_Q // A_KV + 1, q_width=A_Q, kv_width=A_KV,
        radius=A_WINDOW, slopes=slopes_a, sink=p["sink"], emit_lse=False)
    slopes_b = _alibi_slopes(B_HEADS).reshape(len(B_PATTERNS), B_HEADS_PER_GROUP)
    ob, lb = [], []
    for gidx, (src, first) in enumerate((
            (pa, (A_Q + 2 * A_KV) // B_WIDTH),
            (pb1.reshape(n * B_PATTERNS[1][1], T // B_PATTERNS[1][1], GROUP_WIDTH), 0),
            (pb2.reshape(n * B_PATTERNS[2][1], T // B_PATTERNS[2][1], GROUP_WIDTH), 0))):
        dil = B_PATTERNS[gidx][1]
        o, lse = _band_attention(
            src, q_col=first, k_col=first + 1, v_col=first + 2, q_width=B_WIDTH,
            kv_width=B_WIDTH, radius=B_RADIUS, slopes=slopes_b[gidx] * np.float32(dil),
            emit_lse=True)
        ob.append(o.reshape(n, dil, T // dil, B_WIDTH) if dil > 1 else o)
        lb.append(lse.reshape(n, dil, T // dil, B_WIDTH) if dil > 1 else lse)
    x = _mid(x, oa, ob[0], lb[0], ob[1], lb[1], ob[2], lb[2], p["wo_ab"], p["g_ffn0"],
             p["w1"][0], p["w3"][0], p["w2"][0])
    cos, sin = _rope_tables(T)
    qt, k, vt = _proj_c(x, p["g_mix1"], p["wt_c"], p["gq"], p["gk"],
                        cos * (QK_SCALE * LOG2E), sin * (QK_SCALE * LOG2E), cos, sin)
    o = _flash(qt, k, vt)
    return _out(x, o, p["wo_c"], p["g_ffn1"], p["w1"][1], p["w3"][1], p["w2"][1], p["g_final"])


def kernel(x_prompt, x_sample, norm_mix, w_in_ab, w_out_ab, sink_a, w_in_c, w_out_c, q_gain_c,
           k_gain_c, norm_ffn, ffn_w1, ffn_w3, ffn_w2, final_norm):
    assert norm_mix.shape[0] == 2 and all(window // (2 * dil) == B_RADIUS
                                          for window, dil in B_PATTERNS)
    p = _prepare(norm_mix, w_in_ab, w_out_ab, sink_a, w_in_c, w_out_c, q_gain_c, k_gain_c,
                 norm_ffn, ffn_w1, ffn_w3, ffn_w2, final_norm)
    return _trunk(x_prompt, p), _trunk(x_sample, p)
```

```python
import functools

import numpy as np
import jax
import jax.numpy as jnp
from jax import lax
from jax.experimental import pallas as pl
from jax.experimental.pallas import tpu as pltpu

F32 = jnp.float32
BF16 = jnp.bfloat16

D_MODEL = 1024
HEAD_DIM = 64
EPS = 1e-6
NEG = -1e30
A_HEADS = 8
A_KV_HEADS = 2
A_WINDOW = 128
B_PATTERNS = ((128, 1), (512, 4), (2048, 16))
B_HEADS_PER_GROUP = 4
B_HEADS = B_HEADS_PER_GROUP * len(B_PATTERNS)
B_RADIUS = 64
B_WIDTH = B_HEADS_PER_GROUP * HEAD_DIM
C_HEADS = 16
C_KV_HEADS = 4
C_GROUP = C_HEADS // C_KV_HEADS
ROPE_THETA = 10000.0
GRID_W = 64
D_FF = 2816
A_Q = A_HEADS * HEAD_DIM
A_KV = A_KV_HEADS * HEAD_DIM
PA_WIDTH = A_Q + 2 * A_KV + 3 * B_WIDTH
GROUP_WIDTH = 3 * B_WIDTH
AB_OUT = A_Q + B_WIDTH
C_Q = C_HEADS * HEAD_DIM
C_KV = C_KV_HEADS * HEAD_DIM
C_IN = C_Q + 2 * C_KV
QK_SCALE = HEAD_DIM ** -0.5
LOG2E = 1.4426950408889634
V_ROWS = HEAD_DIM + 16
K_COLS = 128
UNDERFLOW_GUARD = 60.0

LANES = 128
ROW_TILE = 512
BAND_TQ = 256
FLASH_TQ = 256
FLASH_TK = ROW_TILE
FLASH_MAX_UNROLL = 8
FF_CHUNKS = ((0, 1536), (1536, D_FF))
VMEM_LIMIT = 56 * 1024 * 1024


def _alibi_slopes(n):
    return np.asarray(2.0 ** (-8.0 * np.arange(1, n + 1) / n), dtype=np.float32)


def _const_spec(shape):
    zeros = (0,) * len(shape)
    return pl.BlockSpec(shape, lambda *_: zeros, pipeline_mode=pl.Buffered(1))


def _params(sem):
    return pltpu.CompilerParams(dimension_semantics=sem, vmem_limit_bytes=VMEM_LIMIT)


def _rms(x, g):
    return x * lax.rsqrt(jnp.mean(x * x, axis=-1, keepdims=True) + EPS) * g


def _ffn(x1, g_ref, w1_ref, w3_ref, w2_ref):
    h = _rms(x1, g_ref[...]).astype(BF16)
    acc = None
    for c0, c1 in FF_CHUNKS:
        u = jnp.dot(h, w1_ref[:, c0:c1], preferred_element_type=F32)
        v = jnp.dot(h, w3_ref[:, c0:c1], preferred_element_type=F32)
        t = (u * (1.0 / (1.0 + jnp.exp(-u))) * v).astype(BF16)
        part = jnp.dot(t, w2_ref[c0:c1, :], preferred_element_type=F32)
        acc = part if acc is None else acc + part
    return x1 + acc


def _proj_ab_kernel(x_ref, g_ref, w_ref, pa_ref, pb1_ref, pb2_ref, slab_ref, *, tm):
    h = _rms(x_ref[...], g_ref[...]).astype(BF16)
    pa_ref[...] = jnp.dot(h, w_ref[:, :PA_WIDTH], preferred_element_type=F32).astype(BF16)
    n_slab = GROUP_WIDTH // LANES
    for (_, dil), out_ref, c0 in ((B_PATTERNS[1], pb1_ref, PA_WIDTH),
                                  (B_PATTERNS[2], pb2_ref, PA_WIDTH + GROUP_WIDTH)):
        r = jnp.dot(h, w_ref[:, c0:c0 + GROUP_WIDTH], preferred_element_type=F32)
        for s in range(n_slab):
            slab_ref[s] = r[:, s * LANES:(s + 1) * LANES]
        rows = tm // dil
        for d in range(dil):
            for s in range(n_slab):
                out_ref[d, :, s * LANES:(s + 1) * LANES] = (
                    slab_ref[s, pl.ds(d, rows, stride=dil), :].astype(BF16))


def _proj_ab(x, g, w):
    n, T, _ = x.shape
    tm = ROW_TILE
    d1, d2 = B_PATTERNS[1][1], B_PATTERNS[2][1]
    return pl.pallas_call(
        functools.partial(_proj_ab_kernel, tm=tm),
        grid=(n, T // tm),
        in_specs=[
            pl.BlockSpec((None, tm, D_MODEL), lambda b, i: (b, i, 0)),
            _const_spec((1, D_MODEL)),
            _const_spec(w.shape),
        ],
        out_specs=[
            pl.BlockSpec((None, tm, PA_WIDTH), lambda b, i: (b, i, 0)),
            pl.BlockSpec((None, d1, tm // d1, GROUP_WIDTH), lambda b, i: (b, 0, i, 0)),
            pl.BlockSpec((None, d2, tm // d2, GROUP_WIDTH), lambda b, i: (b, 0, i, 0)),
        ],
        out_shape=[
            jax.ShapeDtypeStruct((n, T, PA_WIDTH), BF16),
            jax.ShapeDtypeStruct((n, d1, T // d1, GROUP_WIDTH), BF16),
            jax.ShapeDtypeStruct((n, d2, T // d2, GROUP_WIDTH), BF16),
        ],
        scratch_shapes=[pltpu.VMEM((GROUP_WIDTH // LANES, tm, LANES), F32)],
        compiler_params=_params(("parallel", "parallel")),
        name="proj_ab",
    )(x, g, w)


def _band_kernel(*refs, n_q, n_kv, has_sink, emit_lse):
    q_ref, kp_ref, kc_ref, kn_ref, vp_ref, vc_ref, vn_ref, bias_ref = refs[:8]
    pos = 8
    sink_ref = None
    if has_sink:
        sink_ref = refs[pos]
        pos += 1
    o_ref = refs[pos]
    pos += 1
    lse_ref = None
    if emit_lse:
        lse_ref = refs[pos]
        pos += 1
    s_scr, m_scr, ot_scr = refs[pos:pos + 3]
    lt_scr = refs[pos + 3] if emit_lse else None

    k_all = jnp.concatenate([kp_ref[...], kc_ref[...], kn_ref[...]], axis=0)
    v_all = jnp.concatenate([vp_ref[...], vc_ref[...], vn_ref[...]], axis=0)
    width = k_all.shape[0]
    tq = q_ref.shape[0]
    group = n_q // n_kv
    ks = [k_all[:, j * HEAD_DIM:(j + 1) * HEAD_DIM] for j in range(n_kv)]
    vt = v_all.astype(F32).T
    pad = V_ROWS - HEAD_DIM
    ones_row = (lax.broadcasted_iota(jnp.int32, (pad, width), 0) == 0).astype(BF16)
    vts = [jnp.concatenate([vt[j * HEAD_DIM:(j + 1) * HEAD_DIM].astype(BF16), ones_row], axis=0)
           for j in range(n_kv)]
    sinks = [sink_ref[0, h] * LOG2E for h in range(n_q)] if has_sink else None

    def score_phase(h):
        qh = q_ref[:, h * HEAD_DIM:(h + 1) * HEAD_DIM]
        s = lax.dot_general(ks[h // group], qh, (((1,), (1,)), ((), ())),
                            preferred_element_type=F32) + bias_ref[h]
        m = jnp.max(s, axis=0, keepdims=True)
        if has_sink:
            m = jnp.maximum(m, sinks[h])
        s_scr[h % 2] = s
        m_scr[h % 2] = m

    def value_phase(h):
        rows = slice(h * HEAD_DIM, (h + 1) * HEAD_DIM)
        m = m_scr[h % 2]
        p = jnp.exp2(s_scr[h % 2] - m).astype(BF16)
        oe = jnp.dot(vts[h // group], p, preferred_element_type=F32)
        den = oe[HEAD_DIM:HEAD_DIM + 1]
        if has_sink:
            den = den + jnp.exp2(sinks[h] - m)
        ot_scr[rows, :] = oe[:HEAD_DIM] / den
        if emit_lse:
            lt_scr[rows, :] = jnp.broadcast_to(m + jnp.log2(den), (HEAD_DIM, tq))

    score_phase(0)
    for h in range(n_q):
        if h + 1 < n_q:
            score_phase(h + 1)
        value_phase(h)
    o_ref[...] = ot_scr[...].T.astype(o_ref.dtype)
    if emit_lse:
        lse_ref[...] = lt_scr[...].T


def _band_bias(tq, radius, slopes):
    width = tq + 2 * radius
    col = lax.broadcasted_iota(jnp.int32, (width, tq), 0)
    row = lax.broadcasted_iota(jnp.int32, (width, tq), 1)
    dist = jnp.abs(col - radius - row)
    slope2 = jnp.asarray(np.asarray(slopes, np.float32) * np.float32(LOG2E))
    alibi = -slope2[:, None, None] * dist.astype(F32)[None]
    tables = []
    for cls in range(4):
        ok = dist <= radius
        if cls & 1:
            ok = ok & (col >= radius)
        if cls & 2:
            ok = ok & (col < radius + tq)
        tables.append(jnp.where(ok[None], alibi, NEG))
    return jnp.stack(tables, axis=0)


def _band_attention(src, *, q_col, k_col, v_col, q_width, kv_width, radius, slopes,
                    sink=None, emit_lse):
    nb, seq, _ = src.shape
    tq = min(BAND_TQ, seq)
    rb = tq // radius
    n_blk = seq // tq
    last = seq // radius - 1
    n_q = q_width // HEAD_DIM
    n_kv = kv_width // HEAD_DIM
    width = tq + 2 * radius

    def cur(col):
        return lambda i, b: (b, i, col)

    def prev(col):
        return lambda i, b: (b, jnp.maximum(i * rb - 1, 0), col)

    def nxt(col):
        return lambda i, b: (b, jnp.minimum((i + 1) * rb, last), col)

    def edge_class(i, b):
        cls = (i == 0).astype(jnp.int32) + 2 * (i == n_blk - 1).astype(jnp.int32)
        return (cls, 0, 0, 0)

    in_specs = [pl.BlockSpec((None, tq, q_width), cur(q_col))]
    for col in (k_col, v_col):
        in_specs += [
            pl.BlockSpec((None, radius, kv_width), prev(col)),
            pl.BlockSpec((None, tq, kv_width), cur(col)),
            pl.BlockSpec((None, radius, kv_width), nxt(col)),
        ]
    in_specs.append(pl.BlockSpec((None, n_q, width, tq), edge_class))
    args = [src] * 7 + [_band_bias(tq, radius, slopes)]
    if sink is not None:
        in_specs.append(pl.BlockSpec(memory_space=pltpu.SMEM))
        args.append(sink)
    out_dtype = F32 if emit_lse else BF16
    out_specs = [pl.BlockSpec((None, tq, q_width), lambda i, b: (b, i, 0))]
    out_shape = [jax.ShapeDtypeStruct((nb, seq, q_width), out_dtype)]
    if emit_lse:
        out_specs.append(pl.BlockSpec((None, tq, q_width), lambda i, b: (b, i, 0)))
        out_shape.append(jax.ShapeDtypeStruct((nb, seq, q_width), F32))
    kernel = functools.partial(
        _band_kernel, n_q=n_q, n_kv=n_kv, has_sink=sink is not None, emit_lse=emit_lse)
    return pl.pallas_call(
        kernel,
        grid=(n_blk, nb),
        in_specs=in_specs,
        out_specs=out_specs,
        out_shape=out_shape,
        scratch_shapes=[pltpu.VMEM((2, width, tq), F32), pltpu.VMEM((2, 1, tq), F32)]
        + [pltpu.VMEM((q_width, tq), F32)] * (2 if emit_lse else 1),
        compiler_params=_params(("parallel", "parallel")),
        name="band_attention",
    )(*args)


def _mid_kernel(x_ref, oa_ref, o0_ref, l0_ref, o1_ref, l1_ref, o2_ref, l2_ref, wo_ref,
                g_ref, w1_ref, w3_ref, w2_ref, out_ref, slab_ref, *, tm):
    n_slab = B_WIDTH // LANES
    for base, src, (_, dil) in ((0, o1_ref, B_PATTERNS[1]), (n_slab, l1_ref, B_PATTERNS[1]),
                                (2 * n_slab, o2_ref, B_PATTERNS[2]),
                                (3 * n_slab, l2_ref, B_PATTERNS[2])):
        rows = tm // dil
        for d in range(dil):
            for s in range(n_slab):
                slab_ref[base + s, pl.ds(d, rows, stride=dil), :] = (
                    src[d, :, s * LANES:(s + 1) * LANES])
    mixed = []
    for s in range(n_slab):
        sl = slice(s * LANES, (s + 1) * LANES)
        l0, l1, l2 = l0_ref[:, sl], slab_ref[n_slab + s], slab_ref[3 * n_slab + s]
        mx = jnp.maximum(jnp.maximum(l0, l1), l2)
        e0, e1, e2 = jnp.exp2(l0 - mx), jnp.exp2(l1 - mx), jnp.exp2(l2 - mx)
        num = e0 * o0_ref[:, sl] + e1 * slab_ref[s] + e2 * slab_ref[2 * n_slab + s]
        mixed.append(num / (e0 + e1 + e2))
    ob = jnp.concatenate(mixed, axis=-1).astype(BF16)
    attn = (jnp.dot(oa_ref[...], wo_ref[:A_Q, :], preferred_element_type=F32)
            + jnp.dot(ob, wo_ref[A_Q:, :], preferred_element_type=F32))
    out_ref[...] = _ffn(x_ref[...] + attn, g_ref, w1_ref, w3_ref, w2_ref)


def _mid(x, oa, o0, l0, o1, l1, o2, l2, wo, g, w1, w3, w2):
    n, T, _ = x.shape
    tm = ROW_TILE
    d1, d2 = B_PATTERNS[1][1], B_PATTERNS[2][1]
    tok = lambda width: pl.BlockSpec((None, tm, width), lambda b, i: (b, i, 0))
    res = lambda dil: pl.BlockSpec((None, dil, tm // dil, B_WIDTH), lambda b, i: (b, 0, i, 0))
    return pl.pallas_call(
        functools.partial(_mid_kernel, tm=tm),
        grid=(n, T // tm),
        in_specs=[tok(D_MODEL), tok(A_Q), tok(B_WIDTH), tok(B_WIDTH), res(d1), res(d1),
                  res(d2), res(d2), _const_spec(wo.shape), _const_spec((1, D_MODEL)),
                  _const_spec(w1.shape), _const_spec(w3.shape), _const_spec(w2.shape)],
        out_specs=tok(D_MODEL),
        out_shape=jax.ShapeDtypeStruct((n, T, D_MODEL), F32),
        scratch_shapes=[pltpu.VMEM((4 * B_WIDTH // LANES, tm, LANES), F32)],
        compiler_params=_params(("parallel", "parallel")),
        name="mix_out_ffn",
    )(x, oa, o0, l0, o1, l1, o2, l2, wo, g, w1, w3, w2)


def _norm_rope(xh, gain, cos, sin):
    half = HEAD_DIM // 2
    y = xh * lax.rsqrt(jnp.mean(xh * xh, axis=0, keepdims=True) + EPS) * gain
    x0, x1 = y[:half], y[half:]
    return jnp.concatenate([x0 * cos - x1 * sin, x0 * sin + x1 * cos], axis=0)


def _proj_c_kernel(x_ref, g_ref, w_ref, gq_ref, gk_ref, cq_ref, sq_ref, ck_ref, sk_ref,
                   qt_ref, k_ref, vt_ref):
    h = _rms(x_ref[...], g_ref[...]).astype(BF16)
    pt = lax.dot_general(w_ref[...], h, (((1,), (1,)), ((), ())), preferred_element_type=F32)
    gq, gk = gq_ref[...], gk_ref[...]
    cq, sq, ck, sk = cq_ref[...], sq_ref[...], ck_ref[...], sk_ref[...]
    for hd in range(C_HEADS):
        rows = slice(hd * HEAD_DIM, (hd + 1) * HEAD_DIM)
        qt_ref[rows, :] = _norm_rope(pt[rows], gq, cq, sq).astype(BF16)
    one_feature = (lax.broadcasted_iota(jnp.int32, (K_COLS - HEAD_DIM, pt.shape[1]), 0) == 0
                   ).astype(F32)
    pieces = []
    for j in range(C_KV_HEADS):
        pieces += [_norm_rope(pt[C_Q + j * HEAD_DIM:C_Q + (j + 1) * HEAD_DIM], gk, ck, sk),
                   one_feature]
    k_nat = jnp.concatenate(pieces, axis=0).T
    for j in range(C_KV_HEADS):
        k_ref[j] = k_nat[:, j * K_COLS:(j + 1) * K_COLS].astype(BF16)
    pad = V_ROWS - HEAD_DIM
    ones_row = (lax.broadcasted_iota(jnp.int32, (pad, pt.shape[1]), 0) == 0).astype(BF16)
    for j in range(C_KV_HEADS):
        v0 = C_Q + C_KV + j * HEAD_DIM
        vt_ref[j * V_ROWS:j * V_ROWS + HEAD_DIM, :] = pt[v0:v0 + HEAD_DIM].astype(BF16)
        vt_ref[j * V_ROWS + HEAD_DIM:(j + 1) * V_ROWS, :] = ones_row


def _proj_c(x, g, wt, gq, gk, cq, sq, ck, sk):
    n, T, _ = x.shape
    tm = ROW_TILE
    half = HEAD_DIM // 2
    tab = pl.BlockSpec((half, tm), lambda b, i: (0, i))
    return pl.pallas_call(
        _proj_c_kernel,
        grid=(n, T // tm),
        in_specs=[
            pl.BlockSpec((None, tm, D_MODEL), lambda b, i: (b, i, 0)),
            _const_spec((1, D_MODEL)),
            _const_spec(wt.shape),
            _const_spec((HEAD_DIM, tm)),
            _const_spec((HEAD_DIM, tm)),
            tab, tab, tab, tab,
        ],
        out_specs=[
            pl.BlockSpec((None, C_Q, tm), lambda b, i: (b, 0, i)),
            pl.BlockSpec((None, C_KV_HEADS, tm, K_COLS), lambda b, i: (b, 0, i, 0)),
            pl.BlockSpec((None, None, C_KV_HEADS * V_ROWS, tm), lambda b, i: (b, i, 0, 0)),
        ],
        out_shape=[
            jax.ShapeDtypeStruct((n, C_Q, T), BF16),
            jax.ShapeDtypeStruct((n, C_KV_HEADS, T, K_COLS), BF16),
            jax.ShapeDtypeStruct((n, T // tm, C_KV_HEADS * V_ROWS, tm), BF16),
        ],
        compiler_params=_params(("parallel", "parallel")),
        name="proj_c",
    )(x, g, wt, gq, gk, cq, sq, ck, sk)


def _flash_kernel(bound_ref, qt_ref, k_ref, vt_ref, o_ref, sa_ref, sb_ref, qx_ref, mt_ref,
                  acc_ref,
                  *, tq, tk, n_chunks, unroll):
    ref_tile = V_ROWS - HEAD_DIM
    first_row = lax.broadcasted_iota(jnp.int32, (ref_tile, tq), 0) == 0
    for hh in range(C_GROUP):
        qx_ref[hh, :HEAD_DIM, :] = qt_ref[hh * HEAD_DIM:(hh + 1) * HEAD_DIM, :]
        qx_ref[hh, V_ROWS:, :] = jnp.zeros((K_COLS - V_ROWS, tq), BF16)
    mt_ref[...] = jnp.zeros_like(mt_ref)
    bound = jnp.full((1, tq), bound_ref[0, 0], F32).astype(BF16).astype(F32)

    def scores(j, s_ref):
        kb = k_ref[pl.ds(pl.multiple_of(j * tk, tk), tk), :]
        for hh in range(C_GROUP):
            s = jnp.dot(kb, qx_ref[hh], preferred_element_type=F32)
            s_ref[hh] = s
            mt_ref[hh] = jnp.maximum(mt_ref[hh], jnp.max(s, axis=0, keepdims=True))

    def accumulate(j, s_ref):
        vb = vt_ref[j]
        for hh in range(C_GROUP):
            acc_ref[hh] += jnp.dot(vb, jnp.exp2(s_ref[hh]).astype(BF16),
                                   preferred_element_type=F32)

    bufs = (sa_ref, sb_ref)

    def stage(j, parity):
        scores(j + 1, bufs[1 - parity])
        accumulate(j, bufs[parity])

    def trip(t, carry):
        for u in range(unroll):
            stage(t * unroll + u, u % 2)
        return carry

    def one_pass(state):
        n_done, _ = state
        worst = None
        for hh in range(C_GROUP):
            ref = jnp.where(n_done == 0, bound, bound + mt_ref[hh])
            qx_ref[hh, HEAD_DIM:V_ROWS, :] = jnp.where(first_row, -ref, 0.0).astype(BF16)
        acc_ref[...] = jnp.zeros_like(acc_ref)
        mt_ref[...] = jnp.full_like(mt_ref, NEG)
        scores(0, bufs[0])
        lax.fori_loop(0, n_chunks // unroll - 1, trip, 0)
        for j in range(n_chunks - unroll, n_chunks - 1):
            stage(j, j % 2)
        accumulate(n_chunks - 1, bufs[(n_chunks - 1) % 2])
        for hh in range(C_GROUP):
            low = jnp.min(mt_ref[hh])
            worst = low if worst is None else jnp.minimum(worst, low)
        again = jnp.logical_and(n_done == 0, worst < -UNDERFLOW_GUARD)
        return n_done + 1, again.astype(jnp.int32)

    lax.while_loop(lambda state: state[1] != 0, one_pass, (jnp.int32(0), jnp.int32(1)))
    ot = jnp.concatenate(
        [acc_ref[hh, :HEAD_DIM, :] / acc_ref[hh, HEAD_DIM:HEAD_DIM + 1, :]
         for hh in range(C_GROUP)], axis=0)
    o_ref[...] = ot.T.astype(BF16)


def _flash(qt, k, vt, bound):
    n, _, T = qt.shape
    tq, tk = FLASH_TQ, FLASH_TK
    n_chunks = T // tk
    unroll = min(FLASH_MAX_UNROLL, max(2, n_chunks // 2))
    assert unroll % 2 == 0 and n_chunks % unroll == 0
    width = C_GROUP * HEAD_DIM
    return pl.pallas_call(
        functools.partial(_flash_kernel, tq=tq, tk=tk, n_chunks=n_chunks, unroll=unroll),
        grid=(n, C_KV_HEADS, T // tq),
        in_specs=[
            pl.BlockSpec(memory_space=pltpu.SMEM),
            pl.BlockSpec((None, width, tq), lambda b, g, i: (b, g, i)),
            pl.BlockSpec((None, None, T, K_COLS), lambda b, g, i: (b, g, 0, 0)),
            pl.BlockSpec((None, n_chunks, V_ROWS, tk), lambda b, g, i: (b, 0, g, 0)),
        ],
        out_specs=pl.BlockSpec((None, tq, width), lambda b, g, i: (b, i, g)),
        out_shape=jax.ShapeDtypeStruct((n, T, C_Q), BF16),
        scratch_shapes=[
            pltpu.VMEM((C_GROUP, tk, tq), F32), pltpu.VMEM((C_GROUP, tk, tq), F32),
            pltpu.VMEM((C_GROUP, K_COLS, tq), BF16),
            pltpu.VMEM((C_GROUP, 1, tq), F32),
            pltpu.VMEM((C_GROUP, V_ROWS, tq), F32),
        ],
        compiler_params=_params(("parallel", "parallel", "parallel")),
        name="flash_c",
    )(bound, qt, k, vt)


def _out_kernel(x_ref, o_ref, wo_ref, g_ref, w1_ref, w3_ref, w2_ref, gf_ref, out_ref):
    x1 = x_ref[...] + jnp.dot(o_ref[...], wo_ref[...], preferred_element_type=F32)
    out_ref[...] = _rms(_ffn(x1, g_ref, w1_ref, w3_ref, w2_ref), gf_ref[...])


def _out(x, o, wo, g, w1, w3, w2, gf):
    n, T, _ = x.shape
    tm = ROW_TILE
    tok = pl.BlockSpec((None, tm, D_MODEL), lambda b, i: (b, i, 0))
    return pl.pallas_call(
        _out_kernel,
        grid=(n, T // tm),
        in_specs=[tok, tok, _const_spec(wo.shape), _const_spec((1, D_MODEL)),
                  _const_spec(w1.shape), _const_spec(w3.shape), _const_spec(w2.shape),
                  _const_spec((1, D_MODEL))],
        out_specs=tok,
        out_shape=jax.ShapeDtypeStruct((n, T, D_MODEL), F32),
        compiler_params=_params(("parallel", "parallel")),
        name="out_ffn_norm",
    )(x, o, wo, g, w1, w3, w2, gf)


def _rope_tables(T):
    n_freq = HEAD_DIM // 4
    inv = ROPE_THETA ** (-jnp.arange(n_freq, dtype=F32) / n_freq)
    rows = T // GRID_W
    row = jnp.repeat(jnp.arange(rows, dtype=F32), GRID_W)
    col = jnp.tile(jnp.arange(GRID_W, dtype=F32), rows)
    ang = jnp.concatenate([row[:, None] * inv, col[:, None] * inv], axis=-1)
    return jnp.cos(ang).T, jnp.sin(ang).T


def _prepare(norm_mix, w_in_ab, w_out_ab, sink_a, w_in_c, w_out_c, q_gain_c, k_gain_c,
             norm_ffn, ffn_w1, ffn_w3, ffn_w2, final_norm):
    col_scale = np.ones((1, PA_WIDTH + 2 * GROUP_WIDTH), np.float32)
    col_scale[:, :A_Q] = QK_SCALE * LOG2E
    for gidx in range(len(B_PATTERNS)):
        base = A_Q + 2 * A_KV + gidx * GROUP_WIDTH
        col_scale[:, base:base + B_WIDTH] = QK_SCALE * LOG2E
    pair_perm = np.concatenate([np.arange(0, HEAD_DIM, 2), np.arange(1, HEAD_DIM, 2)])
    row_perm = np.concatenate(
        [h * HEAD_DIM + pair_perm for h in range(C_HEADS + C_KV_HEADS)]
        + [np.arange(C_Q + C_KV, C_IN)])
    row = lambda v: v.reshape(1, D_MODEL).astype(F32)
    gain = lambda v: jnp.broadcast_to(v.astype(F32)[pair_perm][:, None], (HEAD_DIM, ROW_TILE))
    return dict(
        g_mix0=row(norm_mix[0]), g_mix1=row(norm_mix[1]),
        g_ffn0=row(norm_ffn[0]), g_ffn1=row(norm_ffn[1]), g_final=row(final_norm),
        w_ab=(w_in_ab[0] * col_scale).astype(BF16),
        wo_ab=w_out_ab[0].astype(BF16),
        sink=sink_a[0].reshape(1, A_HEADS).astype(F32),
        wt_c=w_in_c[0].T[row_perm].astype(BF16),
        wo_c=w_out_c[0].astype(BF16),
        gq=gain(q_gain_c[0]), gk=gain(k_gain_c[0]),
        score_bound=(1.02 * HEAD_DIM * QK_SCALE * LOG2E * jnp.max(jnp.abs(q_gain_c[0]))
                     * jnp.max(jnp.abs(k_gain_c[0]))).astype(F32).reshape(1, 1),
        w1=[ffn_w1[l].astype(BF16) for l in range(2)],
        w3=[ffn_w3[l].astype(BF16) for l in range(2)],
        w2=[ffn_w2[l].astype(BF16) for l in range(2)],
    )


def _trunk(x, p):
    n, T, _ = x.shape
    pa, pb1, pb2 = _proj_ab(x, p["g_mix0"], p["w_ab"])
    slopes_a = _alibi_slopes(A_HEADS)
    (oa,) = _band_attention(
        pa, q_col=0, k_col=A_Q // A_KV, v_col=A_Q // A_KV + 1, q_width=A_Q, kv_width=A_KV,
        radius=A_WINDOW, slopes=slopes_a, sink=p["sink"], emit_lse=False)
    slopes_b = _alibi_slopes(B_HEADS).reshape(len(B_PATTERNS), B_HEADS_PER_GROUP)
    ob, lb = [], []
    for gidx, (src, first) in enumerate((
            (pa, (A_Q + 2 * A_KV) // B_WIDTH),
            (pb1.reshape(n * B_PATTERNS[1][1], T // B_PATTERNS[1][1], GROUP_WIDTH), 0),
            (pb2.reshape(n * B_PATTERNS[2][1], T // B_PATTERNS[2][1], GROUP_WIDTH), 0))):
        dil = B_PATTERNS[gidx][1]
        o, lse = _band_attention(
            src, q_col=first, k_col=first + 1, v_col=first + 2, q_width=B_WIDTH,
            kv_width=B_WIDTH, radius=B_RADIUS, slopes=slopes_b[gidx] * np.float32(dil),
            emit_lse=True)
        ob.append(o.reshape(n, dil, T // dil, B_WIDTH) if dil > 1 else o)
        lb.append(lse.reshape(n, dil, T // dil, B_WIDTH) if dil > 1 else lse)
    x = _mid(x, oa, ob[0], lb[0], ob[1], lb[1], ob[2], lb[2], p["wo_ab"], p["g_ffn0"],
             p["w1"][0], p["w3"][0], p["w2"][0])
    cos, sin = _rope_tables(T)
    qt, k, vt = _proj_c(x, p["g_mix1"], p["wt_c"], p["gq"], p["gk"],
                        cos * (QK_SCALE * LOG2E), sin * (QK_SCALE * LOG2E), cos, sin)
    o = _flash(qt, k, vt, p["score_bound"])
    return _out(x, o, p["wo_c"], p["g_ffn1"], p["w1"][1], p["w3"][1], p["w2"][1], p["g_final"])


def kernel(x_prompt, x_sample, norm_mix, w_in_ab, w_out_ab, sink_a, w_in_c, w_out_c, q_gain_c,
           k_gain_c, norm_ffn, ffn_w1, ffn_w3, ffn_w2, final_norm):
    assert norm_mix.shape[0] == 2 and all(window // (2 * dil) == B_RADIUS
                                          for window, dil in B_PATTERNS)
    p = _prepare(norm_mix, w_in_ab, w_out_ab, sink_a, w_in_c, w_out_c, q_gain_c, k_gain_c,
                 norm_ffn, ffn_w1, ffn_w3, ffn_w2, final_norm)
    return _trunk(x_prompt, p), _trunk(x_sample, p)
```

```python
import functools

import numpy as np
import jax
import jax.numpy as jnp
from jax import lax
from jax.experimental import pallas as pl
from jax.experimental.pallas import tpu as pltpu

F32 = jnp.float32
BF16 = jnp.bfloat16

D_MODEL = 1024
HEAD_DIM = 64
EPS = 1e-6
NEG = -1e30
A_HEADS = 8
A_KV_HEADS = 2
A_WINDOW = 128
B_PATTERNS = ((128, 1), (512, 4), (2048, 16))
B_HEADS_PER_GROUP = 4
B_HEADS = B_HEADS_PER_GROUP * len(B_PATTERNS)
B_RADIUS = 64
B_WIDTH = B_HEADS_PER_GROUP * HEAD_DIM
C_HEADS = 16
C_KV_HEADS = 4
C_GROUP = C_HEADS // C_KV_HEADS
ROPE_THETA = 10000.0
GRID_W = 64
D_FF = 2816
A_Q = A_HEADS * HEAD_DIM
A_KV = A_KV_HEADS * HEAD_DIM
PA_WIDTH = A_Q + 2 * A_KV + 3 * B_WIDTH
GROUP_WIDTH = 3 * B_WIDTH
AB_OUT = A_Q + B_WIDTH
C_Q = C_HEADS * HEAD_DIM
C_KV = C_KV_HEADS * HEAD_DIM
C_IN = C_Q + 2 * C_KV
QK_SCALE = HEAD_DIM ** -0.5
LOG2E = 1.4426950408889634
V_ROWS = HEAD_DIM + 16
K_COLS = 128
UNDERFLOW_GUARD = 60.0

LANES = 128
ROW_TILE = 512
BAND_TQ = 256
BAND_SEQS = 4
FLASH_TQ = 256
FLASH_TK = ROW_TILE
FLASH_MAX_UNROLL = 8
FF_CHUNKS = ((0, 1536), (1536, D_FF))
VMEM_LIMIT = 56 * 1024 * 1024


def _alibi_slopes(n):
    return np.asarray(2.0 ** (-8.0 * np.arange(1, n + 1) / n), dtype=np.float32)


def _const_spec(shape):
    zeros = (0,) * len(shape)
    return pl.BlockSpec(shape, lambda *_: zeros, pipeline_mode=pl.Buffered(1))


def _params(sem):
    return pltpu.CompilerParams(dimension_semantics=sem, vmem_limit_bytes=VMEM_LIMIT)


def _rms(x, g):
    return x * lax.rsqrt(jnp.mean(x * x, axis=-1, keepdims=True) + EPS) * g


def _ffn(x1, g_ref, w1_ref, w3_ref, w2_ref):
    h = _rms(x1, g_ref[...]).astype(BF16)
    acc = None
    for c0, c1 in FF_CHUNKS:
        u = jnp.dot(h, w1_ref[:, c0:c1], preferred_element_type=F32)
        v = jnp.dot(h, w3_ref[:, c0:c1], preferred_element_type=F32)
        t = (u * (1.0 / (1.0 + jnp.exp(-u))) * v).astype(BF16)
        part = jnp.dot(t, w2_ref[c0:c1, :], preferred_element_type=F32)
        acc = part if acc is None else acc + part
    return x1 + acc


def _proj_ab_kernel(x_ref, g_ref, w_ref, pa_ref, pb1_ref, pb2_ref, slab_ref, *, tm):
    h = _rms(x_ref[...], g_ref[...]).astype(BF16)
    pa_ref[...] = jnp.dot(h, w_ref[:, :PA_WIDTH], preferred_element_type=F32).astype(BF16)
    n_slab = GROUP_WIDTH // LANES
    for (_, dil), out_ref, c0 in ((B_PATTERNS[1], pb1_ref, PA_WIDTH),
                                  (B_PATTERNS[2], pb2_ref, PA_WIDTH + GROUP_WIDTH)):
        r = jnp.dot(h, w_ref[:, c0:c0 + GROUP_WIDTH], preferred_element_type=F32)
        for s in range(n_slab):
            slab_ref[s] = r[:, s * LANES:(s + 1) * LANES]
        rows = tm // dil
        for d in range(dil):
            for s in range(n_slab):
                out_ref[d, :, s * LANES:(s + 1) * LANES] = (
                    slab_ref[s, pl.ds(d, rows, stride=dil), :].astype(BF16))


def _proj_ab(x, g, w):
    n, T, _ = x.shape
    tm = ROW_TILE
    d1, d2 = B_PATTERNS[1][1], B_PATTERNS[2][1]
    return pl.pallas_call(
        functools.partial(_proj_ab_kernel, tm=tm),
        grid=(n, T // tm),
        in_specs=[
            pl.BlockSpec((None, tm, D_MODEL), lambda b, i: (b, i, 0)),
            _const_spec((1, D_MODEL)),
            _const_spec(w.shape),
        ],
        out_specs=[
            pl.BlockSpec((None, tm, PA_WIDTH), lambda b, i: (b, i, 0)),
            pl.BlockSpec((None, d1, tm // d1, GROUP_WIDTH), lambda b, i: (b, 0, i, 0)),
            pl.BlockSpec((None, d2, tm // d2, GROUP_WIDTH), lambda b, i: (b, 0, i, 0)),
        ],
        out_shape=[
            jax.ShapeDtypeStruct((n, T, PA_WIDTH), BF16),
            jax.ShapeDtypeStruct((n, d1, T // d1, GROUP_WIDTH), BF16),
            jax.ShapeDtypeStruct((n, d2, T // d2, GROUP_WIDTH), BF16),
        ],
        scratch_shapes=[pltpu.VMEM((GROUP_WIDTH // LANES, tm, LANES), F32)],
        compiler_params=_params(("parallel", "parallel")),
        name="proj_ab",
    )(x, g, w)


def _band_kernel(*refs, n_seq, n_q, n_kv, has_sink, emit_lse):
    q_ref, kp_ref, kc_ref, kn_ref, vp_ref, vc_ref, vn_ref, bias_ref = refs[:8]
    pos = 8
    sink_ref = None
    if has_sink:
        sink_ref = refs[pos]
        pos += 1
    o_ref = refs[pos]
    pos += 1
    lse_ref = None
    if emit_lse:
        lse_ref = refs[pos]
        pos += 1
    s_scr, m_scr, ot_scr = refs[pos:pos + 3]
    lt_scr = refs[pos + 3] if emit_lse else None

    tq = q_ref.shape[1]
    width = kc_ref.shape[1] + 2 * kp_ref.shape[1]
    group = n_q // n_kv
    pad = V_ROWS - HEAD_DIM
    ones_row = (lax.broadcasted_iota(jnp.int32, (pad, width), 0) == 0).astype(BF16)
    ks, vts = [], []
    for b in range(n_seq):
        k_all = jnp.concatenate([kp_ref[b], kc_ref[b], kn_ref[b]], axis=0)
        v_all = jnp.concatenate([vp_ref[b], vc_ref[b], vn_ref[b]], axis=0)
        vt = v_all.astype(F32).T
        ks.append([k_all[:, j * HEAD_DIM:(j + 1) * HEAD_DIM] for j in range(n_kv)])
        vts.append([jnp.concatenate([vt[j * HEAD_DIM:(j + 1) * HEAD_DIM].astype(BF16), ones_row],
                                    axis=0) for j in range(n_kv)])
    sinks = [sink_ref[0, h] * LOG2E for h in range(n_q)] if has_sink else None
    units = [(b, h) for b in range(n_seq) for h in range(n_q)]

    def score_phase(u):
        b, h = units[u]
        qh = q_ref[b, :, h * HEAD_DIM:(h + 1) * HEAD_DIM]
        s = lax.dot_general(ks[b][h // group], qh, (((1,), (1,)), ((), ())),
                            preferred_element_type=F32) + bias_ref[h]
        m = jnp.max(s, axis=0, keepdims=True)
        if has_sink:
            m = jnp.maximum(m, sinks[h])
        s_scr[u % 2] = s
        m_scr[u % 2] = m

    def value_phase(u):
        b, h = units[u]
        rows = slice(h * HEAD_DIM, (h + 1) * HEAD_DIM)
        m = m_scr[u % 2]
        p = jnp.exp2(s_scr[u % 2] - m).astype(BF16)
        oe = jnp.dot(vts[b][h // group], p, preferred_element_type=F32)
        den = oe[HEAD_DIM:HEAD_DIM + 1]
        if has_sink:
            den = den + jnp.exp2(sinks[h] - m)
        ot_scr[b, rows, :] = oe[:HEAD_DIM] / den
        if emit_lse:
            lt_scr[b, rows, :] = jnp.broadcast_to(m + jnp.log2(den), (HEAD_DIM, tq))

    score_phase(0)
    for u in range(len(units)):
        if u + 1 < len(units):
            score_phase(u + 1)
        value_phase(u)
    for b in range(n_seq):
        o_ref[b] = ot_scr[b].T.astype(o_ref.dtype)
        if emit_lse:
            lse_ref[b] = lt_scr[b].T


def _band_bias(tq, radius, slopes):
    width = tq + 2 * radius
    col = lax.broadcasted_iota(jnp.int32, (width, tq), 0)
    row = lax.broadcasted_iota(jnp.int32, (width, tq), 1)
    dist = jnp.abs(col - radius - row)
    slope2 = jnp.asarray(np.asarray(slopes, np.float32) * np.float32(LOG2E))
    alibi = -slope2[:, None, None] * dist.astype(F32)[None]
    tables = []
    for cls in range(4):
        ok = dist <= radius
        if cls & 1:
            ok = ok & (col >= radius)
        if cls & 2:
            ok = ok & (col < radius + tq)
        tables.append(jnp.where(ok[None], alibi, NEG))
    return jnp.stack(tables, axis=0)


def _band_attention(src, *, q_col, k_col, v_col, q_width, kv_width, radius, slopes,
                    sink=None, emit_lse):
    nb, seq, _ = src.shape
    tq = min(BAND_TQ, seq)
    rb = tq // radius
    n_blk = seq // tq
    last = seq // radius - 1
    n_q = q_width // HEAD_DIM
    n_kv = kv_width // HEAD_DIM
    width = tq + 2 * radius

    n_seq = min(BAND_SEQS, nb)
    assert nb % n_seq == 0

    def cur(col):
        return lambda i, b: (b, i, col)

    def prev(col):
        return lambda i, b: (b, jnp.maximum(i * rb - 1, 0), col)

    def nxt(col):
        return lambda i, b: (b, jnp.minimum((i + 1) * rb, last), col)

    def edge_class(i, b):
        cls = (i == 0).astype(jnp.int32) + 2 * (i == n_blk - 1).astype(jnp.int32)
        return (cls, 0, 0, 0)

    in_specs = [pl.BlockSpec((n_seq, tq, q_width), cur(q_col))]
    for col in (k_col, v_col):
        in_specs += [
            pl.BlockSpec((n_seq, radius, kv_width), prev(col)),
            pl.BlockSpec((n_seq, tq, kv_width), cur(col)),
            pl.BlockSpec((n_seq, radius, kv_width), nxt(col)),
        ]
    in_specs.append(pl.BlockSpec((None, n_q, width, tq), edge_class))
    args = [src] * 7 + [_band_bias(tq, radius, slopes)]
    if sink is not None:
        in_specs.append(pl.BlockSpec(memory_space=pltpu.SMEM))
        args.append(sink)
    out_dtype = F32 if emit_lse else BF16
    out_specs = [pl.BlockSpec((n_seq, tq, q_width), lambda i, b: (b, i, 0))]
    out_shape = [jax.ShapeDtypeStruct((nb, seq, q_width), out_dtype)]
    if emit_lse:
        out_specs.append(pl.BlockSpec((n_seq, tq, q_width), lambda i, b: (b, i, 0)))
        out_shape.append(jax.ShapeDtypeStruct((nb, seq, q_width), F32))
    kernel = functools.partial(
        _band_kernel, n_seq=n_seq, n_q=n_q, n_kv=n_kv, has_sink=sink is not None,
        emit_lse=emit_lse)
    return pl.pallas_call(
        kernel,
        grid=(n_blk, nb // n_seq),
        in_specs=in_specs,
        out_specs=out_specs,
        out_shape=out_shape,
        scratch_shapes=[pltpu.VMEM((2, width, tq), F32), pltpu.VMEM((2, 1, tq), F32)]
        + [pltpu.VMEM((n_seq, q_width, tq), F32)] * (2 if emit_lse else 1),
        compiler_params=_params(("parallel", "parallel")),
        name="band_attention",
    )(*args)


def _mid_kernel(x_ref, oa_ref, o0_ref, l0_ref, o1_ref, l1_ref, o2_ref, l2_ref, wo_ref,
                g_ref, w1_ref, w3_ref, w2_ref, out_ref, slab_ref, *, tm):
    n_slab = B_WIDTH // LANES
    for base, src, (_, dil) in ((0, o1_ref, B_PATTERNS[1]), (n_slab, l1_ref, B_PATTERNS[1]),
                                (2 * n_slab, o2_ref, B_PATTERNS[2]),
                                (3 * n_slab, l2_ref, B_PATTERNS[2])):
        rows = tm // dil
        for d in range(dil):
            for s in range(n_slab):
                slab_ref[base + s, pl.ds(d, rows, stride=dil), :] = (
                    src[d, :, s * LANES:(s + 1) * LANES])
    mixed = []
    for s in range(n_slab):
        sl = slice(s * LANES, (s + 1) * LANES)
        l0, l1, l2 = l0_ref[:, sl], slab_ref[n_slab + s], slab_ref[3 * n_slab + s]
        mx = jnp.maximum(jnp.maximum(l0, l1), l2)
        e0, e1, e2 = jnp.exp2(l0 - mx), jnp.exp2(l1 - mx), jnp.exp2(l2 - mx)
        num = e0 * o0_ref[:, sl] + e1 * slab_ref[s] + e2 * slab_ref[2 * n_slab + s]
        mixed.append(num / (e0 + e1 + e2))
    ob = jnp.concatenate(mixed, axis=-1).astype(BF16)
    attn = (jnp.dot(oa_ref[...], wo_ref[:A_Q, :], preferred_element_type=F32)
            + jnp.dot(ob, wo_ref[A_Q:, :], preferred_element_type=F32))
    out_ref[...] = _ffn(x_ref[...] + attn, g_ref, w1_ref, w3_ref, w2_ref)


def _mid(x, oa, o0, l0, o1, l1, o2, l2, wo, g, w1, w3, w2):
    n, T, _ = x.shape
    tm = ROW_TILE
    d1, d2 = B_PATTERNS[1][1], B_PATTERNS[2][1]
    tok = lambda width: pl.BlockSpec((None, tm, width), lambda b, i: (b, i, 0))
    res = lambda dil: pl.BlockSpec((None, dil, tm // dil, B_WIDTH), lambda b, i: (b, 0, i, 0))
    return pl.pallas_call(
        functools.partial(_mid_kernel, tm=tm),
        grid=(n, T // tm),
        in_specs=[tok(D_MODEL), tok(A_Q), tok(B_WIDTH), tok(B_WIDTH), res(d1), res(d1),
                  res(d2), res(d2), _const_spec(wo.shape), _const_spec((1, D_MODEL)),
                  _const_spec(w1.shape), _const_spec(w3.shape), _const_spec(w2.shape)],
        out_specs=tok(D_MODEL),
        out_shape=jax.ShapeDtypeStruct((n, T, D_MODEL), F32),
        scratch_shapes=[pltpu.VMEM((4 * B_WIDTH // LANES, tm, LANES), F32)],
        compiler_params=_params(("parallel", "parallel")),
        name="mix_out_ffn",
    )(x, oa, o0, l0, o1, l1, o2, l2, wo, g, w1, w3, w2)


def _norm_rope(xh, gain, cos, sin):
    half = HEAD_DIM // 2
    y = xh * lax.rsqrt(jnp.mean(xh * xh, axis=0, keepdims=True) + EPS) * gain
    x0, x1 = y[:half], y[half:]
    return jnp.concatenate([x0 * cos - x1 * sin, x0 * sin + x1 * cos], axis=0)


def _proj_c_kernel(x_ref, g_ref, w_ref, gq_ref, gk_ref, cq_ref, sq_ref, ck_ref, sk_ref,
                   qt_ref, k_ref, vt_ref):
    h = _rms(x_ref[...], g_ref[...]).astype(BF16)
    pt = lax.dot_general(w_ref[...], h, (((1,), (1,)), ((), ())), preferred_element_type=F32)
    gq, gk = gq_ref[...], gk_ref[...]
    cq, sq, ck, sk = cq_ref[...], sq_ref[...], ck_ref[...], sk_ref[...]
    for hd in range(C_HEADS):
        rows = slice(hd * HEAD_DIM, (hd + 1) * HEAD_DIM)
        qt_ref[rows, :] = _norm_rope(pt[rows], gq, cq, sq).astype(BF16)
    one_feature = (lax.broadcasted_iota(jnp.int32, (K_COLS - HEAD_DIM, pt.shape[1]), 0) == 0
                   ).astype(F32)
    pieces = []
    for j in range(C_KV_HEADS):
        pieces += [_norm_rope(pt[C_Q + j * HEAD_DIM:C_Q + (j + 1) * HEAD_DIM], gk, ck, sk),
                   one_feature]
    k_nat = jnp.concatenate(pieces, axis=0).T
    for j in range(C_KV_HEADS):
        k_ref[j] = k_nat[:, j * K_COLS:(j + 1) * K_COLS].astype(BF16)
    pad = V_ROWS - HEAD_DIM
    ones_row = (lax.broadcasted_iota(jnp.int32, (pad, pt.shape[1]), 0) == 0).astype(BF16)
    for j in range(C_KV_HEADS):
        v0 = C_Q + C_KV + j * HEAD_DIM
        vt_ref[j * V_ROWS:j * V_ROWS + HEAD_DIM, :] = pt[v0:v0 + HEAD_DIM].astype(BF16)
        vt_ref[j * V_ROWS + HEAD_DIM:(j + 1) * V_ROWS, :] = ones_row


def _proj_c(x, g, wt, gq, gk, cq, sq, ck, sk):
    n, T, _ = x.shape
    tm = ROW_TILE
    half = HEAD_DIM // 2
    tab = pl.BlockSpec((half, tm), lambda b, i: (0, i))
    return pl.pallas_call(
        _proj_c_kernel,
        grid=(n, T // tm),
        in_specs=[
            pl.BlockSpec((None, tm, D_MODEL), lambda b, i: (b, i, 0)),
            _const_spec((1, D_MODEL)),
            _const_spec(wt.shape),
            _const_spec((HEAD_DIM, tm)),
            _const_spec((HEAD_DIM, tm)),
            tab, tab, tab, tab,
        ],
        out_specs=[
            pl.BlockSpec((None, C_Q, tm), lambda b, i: (b, 0, i)),
            pl.BlockSpec((None, C_KV_HEADS, tm, K_COLS), lambda b, i: (b, 0, i, 0)),
            pl.BlockSpec((None, None, C_KV_HEADS * V_ROWS, tm), lambda b, i: (b, i, 0, 0)),
        ],
        out_shape=[
            jax.ShapeDtypeStruct((n, C_Q, T), BF16),
            jax.ShapeDtypeStruct((n, C_KV_HEADS, T, K_COLS), BF16),
            jax.ShapeDtypeStruct((n, T // tm, C_KV_HEADS * V_ROWS, tm), BF16),
        ],
        compiler_params=_params(("parallel", "parallel")),
        name="proj_c",
    )(x, g, wt, gq, gk, cq, sq, ck, sk)


def _flash_kernel(bound_ref, qt_ref, k_ref, vt_ref, o_ref, sa_ref, sb_ref, qx_ref, mt_ref,
                  acc_ref,
                  *, tq, tk, n_chunks, unroll):
    ref_tile = V_ROWS - HEAD_DIM
    first_row = lax.broadcasted_iota(jnp.int32, (ref_tile, tq), 0) == 0
    for hh in range(C_GROUP):
        qx_ref[hh, :HEAD_DIM, :] = qt_ref[hh * HEAD_DIM:(hh + 1) * HEAD_DIM, :]
        qx_ref[hh, V_ROWS:, :] = jnp.zeros((K_COLS - V_ROWS, tq), BF16)
    mt_ref[...] = jnp.zeros_like(mt_ref)
    bound = jnp.full((1, tq), bound_ref[0, 0], F32).astype(BF16).astype(F32)

    def scores(j, p_ref):
        kb = k_ref[pl.ds(pl.multiple_of(j * tk, tk), tk), :]
        for hh in range(C_GROUP):
            s = jnp.dot(kb, qx_ref[hh], preferred_element_type=F32)
            p_ref[hh] = jnp.exp2(s).astype(BF16)
            mt_ref[hh] = jnp.maximum(mt_ref[hh], jnp.max(s, axis=0, keepdims=True))

    def accumulate(j, p_ref):
        vb = vt_ref[j]
        for hh in range(C_GROUP):
            acc_ref[hh] += jnp.dot(vb, p_ref[hh], preferred_element_type=F32)

    bufs = (sa_ref, sb_ref)

    def stage(j, parity):
        scores(j + 1, bufs[1 - parity])
        accumulate(j, bufs[parity])

    def trip(t, carry):
        for u in range(unroll):
            stage(t * unroll + u, u % 2)
        return carry

    def one_pass(state):
        n_done, _ = state
        worst = None
        for hh in range(C_GROUP):
            ref = jnp.where(n_done == 0, bound, bound + mt_ref[hh])
            qx_ref[hh, HEAD_DIM:V_ROWS, :] = jnp.where(first_row, -ref, 0.0).astype(BF16)
        acc_ref[...] = jnp.zeros_like(acc_ref)
        mt_ref[...] = jnp.full_like(mt_ref, NEG)
        scores(0, bufs[0])
        lax.fori_loop(0, n_chunks // unroll - 1, trip, 0)
        for j in range(n_chunks - unroll, n_chunks - 1):
            stage(j, j % 2)
        accumulate(n_chunks - 1, bufs[(n_chunks - 1) % 2])
        for hh in range(C_GROUP):
            low = jnp.min(mt_ref[hh])
            worst = low if worst is None else jnp.minimum(worst, low)
        again = jnp.logical_and(n_done == 0, worst < -UNDERFLOW_GUARD)
        return n_done + 1, again.astype(jnp.int32)

    lax.while_loop(lambda state: state[1] != 0, one_pass, (jnp.int32(0), jnp.int32(1)))
    ot = jnp.concatenate(
        [acc_ref[hh, :HEAD_DIM, :] / acc_ref[hh, HEAD_DIM:HEAD_DIM + 1, :]
         for hh in range(C_GROUP)], axis=0)
    o_ref[...] = ot.T.astype(BF16)


def _flash(qt, k, vt, bound):
    n, _, T = qt.shape
    tq, tk = FLASH_TQ, FLASH_TK
    n_chunks = T // tk
    unroll = min(FLASH_MAX_UNROLL, max(2, n_chunks // 2))
    assert unroll % 2 == 0 and n_chunks % unroll == 0
    width = C_GROUP * HEAD_DIM
    return pl.pallas_call(
        functools.partial(_flash_kernel, tq=tq, tk=tk, n_chunks=n_chunks, unroll=unroll),
        grid=(n, C_KV_HEADS, T // tq),
        in_specs=[
            pl.BlockSpec(memory_space=pltpu.SMEM),
            pl.BlockSpec((None, width, tq), lambda b, g, i: (b, g, i)),
            pl.BlockSpec((None, None, T, K_COLS), lambda b, g, i: (b, g, 0, 0)),
            pl.BlockSpec((None, n_chunks, V_ROWS, tk), lambda b, g, i: (b, 0, g, 0)),
        ],
        out_specs=pl.BlockSpec((None, tq, width), lambda b, g, i: (b, i, g)),
        out_shape=jax.ShapeDtypeStruct((n, T, C_Q), BF16),
        scratch_shapes=[
            pltpu.VMEM((C_GROUP, tk, tq), BF16), pltpu.VMEM((C_GROUP, tk, tq), BF16),
            pltpu.VMEM((C_GROUP, K_COLS, tq), BF16),
            pltpu.VMEM((C_GROUP, 1, tq), F32),
            pltpu.VMEM((C_GROUP, V_ROWS, tq), F32),
        ],
        compiler_params=_params(("parallel", "parallel", "parallel")),
        name="flash_c",
    )(bound, qt, k, vt)


def _out_kernel(x_ref, o_ref, wo_ref, g_ref, w1_ref, w3_ref, w2_ref, gf_ref, out_ref):
    x1 = x_ref[...] + jnp.dot(o_ref[...], wo_ref[...], preferred_element_type=F32)
    out_ref[...] = _rms(_ffn(x1, g_ref, w1_ref, w3_ref, w2_ref), gf_ref[...])


def _out(x, o, wo, g, w1, w3, w2, gf):
    n, T, _ = x.shape
    tm = ROW_TILE
    tok = pl.BlockSpec((None, tm, D_MODEL), lambda b, i: (b, i, 0))
    return pl.pallas_call(
        _out_kernel,
        grid=(n, T // tm),
        in_specs=[tok, tok, _const_spec(wo.shape), _const_spec((1, D_MODEL)),
                  _const_spec(w1.shape), _const_spec(w3.shape), _const_spec(w2.shape),
                  _const_spec((1, D_MODEL))],
        out_specs=tok,
        out_shape=jax.ShapeDtypeStruct((n, T, D_MODEL), F32),
        compiler_params=_params(("parallel", "parallel")),
        name="out_ffn_norm",
    )(x, o, wo, g, w1, w3, w2, gf)


def _rope_tables(T):
    n_freq = HEAD_DIM // 4
    inv = ROPE_THETA ** (-jnp.arange(n_freq, dtype=F32) / n_freq)
    rows = T // GRID_W
    row = jnp.repeat(jnp.arange(rows, dtype=F32), GRID_W)
    col = jnp.tile(jnp.arange(GRID_W, dtype=F32), rows)
    ang = jnp.concatenate([row[:, None] * inv, col[:, None] * inv], axis=-1)
    return jnp.cos(ang).T, jnp.sin(ang).T


def _prepare(norm_mix, w_in_ab, w_out_ab, sink_a, w_in_c, w_out_c, q_gain_c, k_gain_c,
             norm_ffn, ffn_w1, ffn_w3, ffn_w2, final_norm):
    col_scale = np.ones((1, PA_WIDTH + 2 * GROUP_WIDTH), np.float32)
    col_scale[:, :A_Q] = QK_SCALE * LOG2E
    for gidx in range(len(B_PATTERNS)):
        base = A_Q + 2 * A_KV + gidx * GROUP_WIDTH
        col_scale[:, base:base + B_WIDTH] = QK_SCALE * LOG2E
    pair_perm = np.concatenate([np.arange(0, HEAD_DIM, 2), np.arange(1, HEAD_DIM, 2)])
    row_perm = np.concatenate(
        [h * HEAD_DIM + pair_perm for h in range(C_HEADS + C_KV_HEADS)]
        + [np.arange(C_Q + C_KV, C_IN)])
    row = lambda v: v.reshape(1, D_MODEL).astype(F32)
    gain = lambda v: jnp.broadcast_to(v.astype(F32)[pair_perm][:, None], (HEAD_DIM, ROW_TILE))
    return dict(
        g_mix0=row(norm_mix[0]), g_mix1=row(norm_mix[1]),
        g_ffn0=row(norm_ffn[0]), g_ffn1=row(norm_ffn[1]), g_final=row(final_norm),
        w_ab=(w_in_ab[0] * col_scale).astype(BF16),
        wo_ab=w_out_ab[0].astype(BF16),
        sink=sink_a[0].reshape(1, A_HEADS).astype(F32),
        wt_c=w_in_c[0].T[row_perm].astype(BF16),
        wo_c=w_out_c[0].astype(BF16),
        gq=gain(q_gain_c[0]), gk=gain(k_gain_c[0]),
        score_bound=(1.02 * HEAD_DIM * QK_SCALE * LOG2E * jnp.max(jnp.abs(q_gain_c[0]))
                     * jnp.max(jnp.abs(k_gain_c[0]))).astype(F32).reshape(1, 1),
        w1=[ffn_w1[l].astype(BF16) for l in range(2)],
        w3=[ffn_w3[l].astype(BF16) for l in range(2)],
        w2=[ffn_w2[l].astype(BF16) for l in range(2)],
    )


def _trunk(x, p):
    n, T, _ = x.shape
    pa, pb1, pb2 = _proj_ab(x, p["g_mix0"], p["w_ab"])
    slopes_a = _alibi_slopes(A_HEADS)
    (oa,) = _band_attention(
        pa, q_col=0, k_col=A_Q // A_KV, v_col=A_Q // A_KV + 1, q_width=A_Q, kv_width=A_KV,
        radius=A_WINDOW, slopes=slopes_a, sink=p["sink"], emit_lse=False)
    slopes_b = _alibi_slopes(B_HEADS).reshape(len(B_PATTERNS), B_HEADS_PER_GROUP)
    ob, lb = [], []
    for gidx, (src, first) in enumerate((
            (pa, (A_Q + 2 * A_KV) // B_WIDTH),
            (pb1.reshape(n * B_PATTERNS[1][1], T // B_PATTERNS[1][1], GROUP_WIDTH), 0),
            (pb2.reshape(n * B_PATTERNS[2][1], T // B_PATTERNS[2][1], GROUP_WIDTH), 0))):
        dil = B_PATTERNS[gidx][1]
        o, lse = _band_attention(
            src, q_col=first, k_col=first + 1, v_col=first + 2, q_width=B_WIDTH,
            kv_width=B_WIDTH, radius=B_RADIUS, slopes=slopes_b[gidx] * np.float32(dil),
            emit_lse=True)
        ob.append(o.reshape(n, dil, T // dil, B_WIDTH) if dil > 1 else o)
        lb.append(lse.reshape(n, dil, T // dil, B_WIDTH) if dil > 1 else lse)
    x = _mid(x, oa, ob[0], lb[0], ob[1], lb[1], ob[2], lb[2], p["wo_ab"], p["g_ffn0"],
             p["w1"][0], p["w3"][0], p["w2"][0])
    cos, sin = _rope_tables(T)
    qt, k, vt = _proj_c(x, p["g_mix1"], p["wt_c"], p["gq"], p["gk"],
                        cos * (QK_SCALE * LOG2E), sin * (QK_SCALE * LOG2E), cos, sin)
    o = _flash(qt, k, vt, p["score_bound"])
    return _out(x, o, p["wo_c"], p["g_ffn1"], p["w1"][1], p["w3"][1], p["w2"][1], p["g_final"])


def kernel(x_prompt, x_sample, norm_mix, w_in_ab, w_out_ab, sink_a, w_in_c, w_out_c, q_gain_c,
           k_gain_c, norm_ffn, ffn_w1, ffn_w3, ffn_w2, final_norm):
    assert norm_mix.shape[0] == 2 and all(window // (2 * dil) == B_RADIUS
                                          for window, dil in B_PATTERNS)
    p = _prepare(norm_mix, w_in_ab, w_out_ab, sink_a, w_in_c, w_out_c, q_gain_c, k_gain_c,
                 norm_ffn, ffn_w1, ffn_w3, ffn_w2, final_norm)
    return _trunk(x_prompt, p), _trunk(x_sample, p)
```

```python
import functools

import numpy as np
import jax
import jax.numpy as jnp
from jax import lax
from jax.experimental import pallas as pl
from jax.experimental.pallas import tpu as pltpu

F32 = jnp.float32
BF16 = jnp.bfloat16

D_MODEL = 1024
HEAD_DIM = 64
EPS = 1e-6
NEG = -1e30
A_HEADS = 8
A_KV_HEADS = 2
A_WINDOW = 128
B_PATTERNS = ((128, 1), (512, 4), (2048, 16))
B_HEADS_PER_GROUP = 4
B_HEADS = B_HEADS_PER_GROUP * len(B_PATTERNS)
B_RADIUS = 64
B_WIDTH = B_HEADS_PER_GROUP * HEAD_DIM
C_HEADS = 16
C_KV_HEADS = 4
C_GROUP = C_HEADS // C_KV_HEADS
ROPE_THETA = 10000.0
GRID_W = 64
D_FF = 2816
A_Q = A_HEADS * HEAD_DIM
A_KV = A_KV_HEADS * HEAD_DIM
PA_WIDTH = A_Q + 2 * A_KV + 3 * B_WIDTH
GROUP_WIDTH = 3 * B_WIDTH
AB_OUT = A_Q + B_WIDTH
C_Q = C_HEADS * HEAD_DIM
C_KV = C_KV_HEADS * HEAD_DIM
C_IN = C_Q + 2 * C_KV
QK_SCALE = HEAD_DIM ** -0.5
LOG2E = 1.4426950408889634
V_ROWS = HEAD_DIM + 16
K_COLS = 128
UNDERFLOW_GUARD = 60.0

LANES = 128
ROW_TILE = 512
BAND_TQ = 256
BAND_SEQS = 4
FLASH_TQ = 512
FLASH_TK = ROW_TILE
FLASH_MAX_UNROLL = 8
FF_CHUNKS = ((0, 1536), (1536, D_FF))
VMEM_LIMIT = 56 * 1024 * 1024


def _alibi_slopes(n):
    return np.asarray(2.0 ** (-8.0 * np.arange(1, n + 1) / n), dtype=np.float32)


def _const_spec(shape):
    zeros = (0,) * len(shape)
    return pl.BlockSpec(shape, lambda *_: zeros, pipeline_mode=pl.Buffered(1))


def _params(sem):
    return pltpu.CompilerParams(dimension_semantics=sem, vmem_limit_bytes=VMEM_LIMIT)


def _rms(x, g):
    return x * lax.rsqrt(jnp.mean(x * x, axis=-1, keepdims=True) + EPS) * g


def _ffn(x1, g_ref, w1_ref, w3_ref, w2_ref):
    h = _rms(x1, g_ref[...]).astype(BF16)
    acc = None
    for c0, c1 in FF_CHUNKS:
        u = jnp.dot(h, w1_ref[:, c0:c1], preferred_element_type=F32)
        v = jnp.dot(h, w3_ref[:, c0:c1], preferred_element_type=F32)
        t = (u * (1.0 / (1.0 + jnp.exp(-u))) * v).astype(BF16)
        part = jnp.dot(t, w2_ref[c0:c1, :], preferred_element_type=F32)
        acc = part if acc is None else acc + part
    return x1 + acc


def _proj_ab_kernel(x_ref, g_ref, w_ref, pa_ref, pb1_ref, pb2_ref, slab_ref, *, tm):
    h = _rms(x_ref[...], g_ref[...]).astype(BF16)
    pa_ref[...] = jnp.dot(h, w_ref[:, :PA_WIDTH], preferred_element_type=F32).astype(BF16)
    n_slab = GROUP_WIDTH // LANES
    for (_, dil), out_ref, c0 in ((B_PATTERNS[1], pb1_ref, PA_WIDTH),
                                  (B_PATTERNS[2], pb2_ref, PA_WIDTH + GROUP_WIDTH)):
        r = jnp.dot(h, w_ref[:, c0:c0 + GROUP_WIDTH], preferred_element_type=F32)
        for s in range(n_slab):
            slab_ref[s] = r[:, s * LANES:(s + 1) * LANES]
        rows = tm // dil
        for d in range(dil):
            for s in range(n_slab):
                out_ref[d, :, s * LANES:(s + 1) * LANES] = (
                    slab_ref[s, pl.ds(d, rows, stride=dil), :].astype(BF16))


def _proj_ab(x, g, w):
    n, T, _ = x.shape
    tm = ROW_TILE
    d1, d2 = B_PATTERNS[1][1], B_PATTERNS[2][1]
    return pl.pallas_call(
        functools.partial(_proj_ab_kernel, tm=tm),
        grid=(n, T // tm),
        in_specs=[
            pl.BlockSpec((None, tm, D_MODEL), lambda b, i: (b, i, 0)),
            _const_spec((1, D_MODEL)),
            _const_spec(w.shape),
        ],
        out_specs=[
            pl.BlockSpec((None, tm, PA_WIDTH), lambda b, i: (b, i, 0)),
            pl.BlockSpec((None, d1, tm // d1, GROUP_WIDTH), lambda b, i: (b, 0, i, 0)),
            pl.BlockSpec((None, d2, tm // d2, GROUP_WIDTH), lambda b, i: (b, 0, i, 0)),
        ],
        out_shape=[
            jax.ShapeDtypeStruct((n, T, PA_WIDTH), BF16),
            jax.ShapeDtypeStruct((n, d1, T // d1, GROUP_WIDTH), BF16),
            jax.ShapeDtypeStruct((n, d2, T // d2, GROUP_WIDTH), BF16),
        ],
        scratch_shapes=[pltpu.VMEM((GROUP_WIDTH // LANES, tm, LANES), F32)],
        compiler_params=_params(("parallel", "parallel")),
        name="proj_ab",
    )(x, g, w)


def _band_kernel(*refs, n_seq, n_q, n_kv, has_sink, emit_lse):
    q_ref, kp_ref, kc_ref, kn_ref, vp_ref, vc_ref, vn_ref, bias_ref = refs[:8]
    pos = 8
    sink_ref = None
    if has_sink:
        sink_ref = refs[pos]
        pos += 1
    o_ref = refs[pos]
    pos += 1
    lse_ref = None
    if emit_lse:
        lse_ref = refs[pos]
        pos += 1
    s_scr, m_scr, ot_scr = refs[pos:pos + 3]
    lt_scr = refs[pos + 3] if emit_lse else None

    tq = q_ref.shape[1]
    width = kc_ref.shape[1] + 2 * kp_ref.shape[1]
    group = n_q // n_kv
    pad = V_ROWS - HEAD_DIM
    ones_row = (lax.broadcasted_iota(jnp.int32, (pad, width), 0) == 0).astype(BF16)
    ks, vts = [], []
    for b in range(n_seq):
        k_all = jnp.concatenate([kp_ref[b], kc_ref[b], kn_ref[b]], axis=0)
        v_all = jnp.concatenate([vp_ref[b], vc_ref[b], vn_ref[b]], axis=0)
        vt = v_all.astype(F32).T
        ks.append([k_all[:, j * HEAD_DIM:(j + 1) * HEAD_DIM] for j in range(n_kv)])
        vts.append([jnp.concatenate([vt[j * HEAD_DIM:(j + 1) * HEAD_DIM].astype(BF16), ones_row],
                                    axis=0) for j in range(n_kv)])
    sinks = [sink_ref[0, h] * LOG2E for h in range(n_q)] if has_sink else None
    units = [(b, h) for b in range(n_seq) for h in range(n_q)]

    def score_phase(u):
        b, h = units[u]
        qh = q_ref[b, :, h * HEAD_DIM:(h + 1) * HEAD_DIM]
        s = lax.dot_general(ks[b][h // group], qh, (((1,), (1,)), ((), ())),
                            preferred_element_type=F32) + bias_ref[h]
        m = jnp.max(s, axis=0, keepdims=True)
        if has_sink:
            m = jnp.maximum(m, sinks[h])
        s_scr[u % 2] = s
        m_scr[u % 2] = m

    def value_phase(u):
        b, h = units[u]
        rows = slice(h * HEAD_DIM, (h + 1) * HEAD_DIM)
        m = m_scr[u % 2]
        p = jnp.exp2(s_scr[u % 2] - m).astype(BF16)
        oe = jnp.dot(vts[b][h // group], p, preferred_element_type=F32)
        den = oe[HEAD_DIM:HEAD_DIM + 1]
        if has_sink:
            den = den + jnp.exp2(sinks[h] - m)
        ot_scr[b, rows, :] = oe[:HEAD_DIM] / den
        if emit_lse:
            lt_scr[b, rows, :] = jnp.broadcast_to(m + jnp.log2(den), (HEAD_DIM, tq))

    score_phase(0)
    for u in range(len(units)):
        if u + 1 < len(units):
            score_phase(u + 1)
        value_phase(u)
    for b in range(n_seq):
        o_ref[b] = ot_scr[b].T.astype(o_ref.dtype)
        if emit_lse:
            lse_ref[b] = lt_scr[b].T


def _band_bias(tq, radius, slopes):
    width = tq + 2 * radius
    col = lax.broadcasted_iota(jnp.int32, (width, tq), 0)
    row = lax.broadcasted_iota(jnp.int32, (width, tq), 1)
    dist = jnp.abs(col - radius - row)
    slope2 = jnp.asarray(np.asarray(slopes, np.float32) * np.float32(LOG2E))
    alibi = -slope2[:, None, None] * dist.astype(F32)[None]
    tables = []
    for cls in range(4):
        ok = dist <= radius
        if cls & 1:
            ok = ok & (col >= radius)
        if cls & 2:
            ok = ok & (col < radius + tq)
        tables.append(jnp.where(ok[None], alibi, NEG))
    return jnp.stack(tables, axis=0)


def _band_attention(src, *, q_col, k_col, v_col, q_width, kv_width, radius, slopes,
                    sink=None, emit_lse):
    nb, seq, _ = src.shape
    tq = min(BAND_TQ, seq)
    rb = tq // radius
    n_blk = seq // tq
    last = seq // radius - 1
    n_q = q_width // HEAD_DIM
    n_kv = kv_width // HEAD_DIM
    width = tq + 2 * radius

    n_seq = min(BAND_SEQS, nb)
    assert nb % n_seq == 0

    def cur(col):
        return lambda i, b: (b, i, col)

    def prev(col):
        return lambda i, b: (b, jnp.maximum(i * rb - 1, 0), col)

    def nxt(col):
        return lambda i, b: (b, jnp.minimum((i + 1) * rb, last), col)

    def edge_class(i, b):
        cls = (i == 0).astype(jnp.int32) + 2 * (i == n_blk - 1).astype(jnp.int32)
        return (cls, 0, 0, 0)

    in_specs = [pl.BlockSpec((n_seq, tq, q_width), cur(q_col))]
    for col in (k_col, v_col):
        in_specs += [
            pl.BlockSpec((n_seq, radius, kv_width), prev(col)),
            pl.BlockSpec((n_seq, tq, kv_width), cur(col)),
            pl.BlockSpec((n_seq, radius, kv_width), nxt(col)),
        ]
    in_specs.append(pl.BlockSpec((None, n_q, width, tq), edge_class))
    args = [src] * 7 + [_band_bias(tq, radius, slopes)]
    if sink is not None:
        in_specs.append(pl.BlockSpec(memory_space=pltpu.SMEM))
        args.append(sink)
    out_dtype = F32 if emit_lse else BF16
    out_specs = [pl.BlockSpec((n_seq, tq, q_width), lambda i, b: (b, i, 0))]
    out_shape = [jax.ShapeDtypeStruct((nb, seq, q_width), out_dtype)]
    if emit_lse:
        out_specs.append(pl.BlockSpec((n_seq, tq, q_width), lambda i, b: (b, i, 0)))
        out_shape.append(jax.ShapeDtypeStruct((nb, seq, q_width), F32))
    kernel = functools.partial(
        _band_kernel, n_seq=n_seq, n_q=n_q, n_kv=n_kv, has_sink=sink is not None,
        emit_lse=emit_lse)
    return pl.pallas_call(
        kernel,
        grid=(n_blk, nb // n_seq),
        in_specs=in_specs,
        out_specs=out_specs,
        out_shape=out_shape,
        scratch_shapes=[pltpu.VMEM((2, width, tq), F32), pltpu.VMEM((2, 1, tq), F32)]
        + [pltpu.VMEM((n_seq, q_width, tq), F32)] * (2 if emit_lse else 1),
        compiler_params=_params(("parallel", "parallel")),
        name="band_attention",
    )(*args)


def _mid_kernel(x_ref, oa_ref, o0_ref, l0_ref, o1_ref, l1_ref, o2_ref, l2_ref, wo_ref,
                g_ref, w1_ref, w3_ref, w2_ref, out_ref, slab_ref, *, tm):
    n_slab = B_WIDTH // LANES
    for base, src, (_, dil) in ((0, o1_ref, B_PATTERNS[1]), (n_slab, l1_ref, B_PATTERNS[1]),
                                (2 * n_slab, o2_ref, B_PATTERNS[2]),
                                (3 * n_slab, l2_ref, B_PATTERNS[2])):
        rows = tm // dil
        for d in range(dil):
            for s in range(n_slab):
                slab_ref[base + s, pl.ds(d, rows, stride=dil), :] = (
                    src[d, :, s * LANES:(s + 1) * LANES])
    mixed = []
    for s in range(n_slab):
        sl = slice(s * LANES, (s + 1) * LANES)
        l0, l1, l2 = l0_ref[:, sl], slab_ref[n_slab + s], slab_ref[3 * n_slab + s]
        mx = jnp.maximum(jnp.maximum(l0, l1), l2)
        e0, e1, e2 = jnp.exp2(l0 - mx), jnp.exp2(l1 - mx), jnp.exp2(l2 - mx)
        num = e0 * o0_ref[:, sl] + e1 * slab_ref[s] + e2 * slab_ref[2 * n_slab + s]
        mixed.append(num / (e0 + e1 + e2))
    ob = jnp.concatenate(mixed, axis=-1).astype(BF16)
    attn = (jnp.dot(oa_ref[...], wo_ref[:A_Q, :], preferred_element_type=F32)
            + jnp.dot(ob, wo_ref[A_Q:, :], preferred_element_type=F32))
    out_ref[...] = _ffn(x_ref[...] + attn, g_ref, w1_ref, w3_ref, w2_ref)


def _mid(x, oa, o0, l0, o1, l1, o2, l2, wo, g, w1, w3, w2):
    n, T, _ = x.shape
    tm = ROW_TILE
    d1, d2 = B_PATTERNS[1][1], B_PATTERNS[2][1]
    tok = lambda width: pl.BlockSpec((None, tm, width), lambda b, i: (b, i, 0))
    res = lambda dil: pl.BlockSpec((None, dil, tm // dil, B_WIDTH), lambda b, i: (b, 0, i, 0))
    return pl.pallas_call(
        functools.partial(_mid_kernel, tm=tm),
        grid=(n, T // tm),
        in_specs=[tok(D_MODEL), tok(A_Q), tok(B_WIDTH), tok(B_WIDTH), res(d1), res(d1),
                  res(d2), res(d2), _const_spec(wo.shape), _const_spec((1, D_MODEL)),
                  _const_spec(w1.shape), _const_spec(w3.shape), _const_spec(w2.shape)],
        out_specs=tok(D_MODEL),
        out_shape=jax.ShapeDtypeStruct((n, T, D_MODEL), F32),
        scratch_shapes=[pltpu.VMEM((4 * B_WIDTH // LANES, tm, LANES), F32)],
        compiler_params=_params(("parallel", "parallel")),
        name="mix_out_ffn",
    )(x, oa, o0, l0, o1, l1, o2, l2, wo, g, w1, w3, w2)


def _norm_rope(xh, gain, cos, sin):
    half = HEAD_DIM // 2
    y = xh * lax.rsqrt(jnp.mean(xh * xh, axis=0, keepdims=True) + EPS) * gain
    x0, x1 = y[:half], y[half:]
    return jnp.concatenate([x0 * cos - x1 * sin, x0 * sin + x1 * cos], axis=0)


def _proj_c_kernel(x_ref, g_ref, w_ref, gq_ref, gk_ref, cq_ref, sq_ref, ck_ref, sk_ref,
                   qt_ref, k_ref, vt_ref):
    h = _rms(x_ref[...], g_ref[...]).astype(BF16)
    pt = lax.dot_general(w_ref[...], h, (((1,), (1,)), ((), ())), preferred_element_type=F32)
    gq, gk = gq_ref[...], gk_ref[...]
    cq, sq, ck, sk = cq_ref[...], sq_ref[...], ck_ref[...], sk_ref[...]
    for hd in range(C_HEADS):
        rows = slice(hd * HEAD_DIM, (hd + 1) * HEAD_DIM)
        qt_ref[rows, :] = _norm_rope(pt[rows], gq, cq, sq).astype(BF16)
    one_feature = (lax.broadcasted_iota(jnp.int32, (K_COLS - HEAD_DIM, pt.shape[1]), 0) == 0
                   ).astype(F32)
    pieces = []
    for j in range(C_KV_HEADS):
        pieces += [_norm_rope(pt[C_Q + j * HEAD_DIM:C_Q + (j + 1) * HEAD_DIM], gk, ck, sk),
                   one_feature]
    k_nat = jnp.concatenate(pieces, axis=0).T
    for j in range(C_KV_HEADS):
        k_ref[j] = k_nat[:, j * K_COLS:(j + 1) * K_COLS].astype(BF16)
    pad = V_ROWS - HEAD_DIM
    ones_row = (lax.broadcasted_iota(jnp.int32, (pad, pt.shape[1]), 0) == 0).astype(BF16)
    for j in range(C_KV_HEADS):
        v0 = C_Q + C_KV + j * HEAD_DIM
        vt_ref[j * V_ROWS:j * V_ROWS + HEAD_DIM, :] = pt[v0:v0 + HEAD_DIM].astype(BF16)
        vt_ref[j * V_ROWS + HEAD_DIM:(j + 1) * V_ROWS, :] = ones_row


def _proj_c(x, g, wt, gq, gk, cq, sq, ck, sk):
    n, T, _ = x.shape
    tm = ROW_TILE
    half = HEAD_DIM // 2
    tab = pl.BlockSpec((half, tm), lambda b, i: (0, i))
    return pl.pallas_call(
        _proj_c_kernel,
        grid=(n, T // tm),
        in_specs=[
            pl.BlockSpec((None, tm, D_MODEL), lambda b, i: (b, i, 0)),
            _const_spec((1, D_MODEL)),
            _const_spec(wt.shape),
            _const_spec((HEAD_DIM, tm)),
            _const_spec((HEAD_DIM, tm)),
            tab, tab, tab, tab,
        ],
        out_specs=[
            pl.BlockSpec((None, C_Q, tm), lambda b, i: (b, 0, i)),
            pl.BlockSpec((None, C_KV_HEADS, tm, K_COLS), lambda b, i: (b, 0, i, 0)),
            pl.BlockSpec((None, None, C_KV_HEADS * V_ROWS, tm), lambda b, i: (b, i, 0, 0)),
        ],
        out_shape=[
            jax.ShapeDtypeStruct((n, C_Q, T), BF16),
            jax.ShapeDtypeStruct((n, C_KV_HEADS, T, K_COLS), BF16),
            jax.ShapeDtypeStruct((n, T // tm, C_KV_HEADS * V_ROWS, tm), BF16),
        ],
        compiler_params=_params(("parallel", "parallel")),
        name="proj_c",
    )(x, g, wt, gq, gk, cq, sq, ck, sk)


def _flash_kernel(bound_ref, qt_ref, k_ref, vt_ref, o_ref, sa_ref, sb_ref, qx_ref, mt_ref,
                  acc_ref,
                  *, tq, tk, n_chunks, unroll):
    ref_tile = V_ROWS - HEAD_DIM
    first_row = lax.broadcasted_iota(jnp.int32, (ref_tile, tq), 0) == 0
    for hh in range(C_GROUP):
        qx_ref[hh, :HEAD_DIM, :] = qt_ref[hh * HEAD_DIM:(hh + 1) * HEAD_DIM, :]
        qx_ref[hh, V_ROWS:, :] = jnp.zeros((K_COLS - V_ROWS, tq), BF16)
    mt_ref[...] = jnp.zeros_like(mt_ref)
    bound = jnp.full((1, tq), bound_ref[0, 0], F32).astype(BF16).astype(F32)

    def scores(j, p_ref):
        kb = k_ref[pl.ds(pl.multiple_of(j * tk, tk), tk), :]
        for hh in range(C_GROUP):
            s = jnp.dot(kb, qx_ref[hh], preferred_element_type=F32)
            p_ref[hh] = jnp.exp2(s).astype(BF16)
            mt_ref[hh] = jnp.maximum(mt_ref[hh], jnp.max(s, axis=0, keepdims=True))

    def accumulate(j, p_ref):
        vb = vt_ref[j]
        for hh in range(C_GROUP):
            acc_ref[hh] += jnp.dot(vb, p_ref[hh], preferred_element_type=F32)

    bufs = (sa_ref, sb_ref)

    def stage(j, parity):
        scores(j + 1, bufs[1 - parity])
        accumulate(j, bufs[parity])

    def trip(t, carry):
        for u in range(unroll):
            stage(t * unroll + u, u % 2)
        return carry

    def one_pass(state):
        n_done, _ = state
        worst = None
        for hh in range(C_GROUP):
            ref = jnp.where(n_done == 0, bound, bound + mt_ref[hh])
            qx_ref[hh, HEAD_DIM:V_ROWS, :] = jnp.where(first_row, -ref, 0.0).astype(BF16)
        acc_ref[...] = jnp.zeros_like(acc_ref)
        mt_ref[...] = jnp.full_like(mt_ref, NEG)
        scores(0, bufs[0])
        lax.fori_loop(0, n_chunks // unroll - 1, trip, 0)
        for j in range(n_chunks - unroll, n_chunks - 1):
            stage(j, j % 2)
        accumulate(n_chunks - 1, bufs[(n_chunks - 1) % 2])
        for hh in range(C_GROUP):
            low = jnp.min(mt_ref[hh])
            worst = low if worst is None else jnp.minimum(worst, low)
        again = jnp.logical_and(n_done == 0, worst < -UNDERFLOW_GUARD)
        return n_done + 1, again.astype(jnp.int32)

    lax.while_loop(lambda state: state[1] != 0, one_pass, (jnp.int32(0), jnp.int32(1)))
    ot = jnp.concatenate(
        [acc_ref[hh, :HEAD_DIM, :] / acc_ref[hh, HEAD_DIM:HEAD_DIM + 1, :]
         for hh in range(C_GROUP)], axis=0)
    o_ref[...] = ot.T.astype(BF16)


def _flash(qt, k, vt, bound):
    n, _, T = qt.shape
    tq, tk = FLASH_TQ, FLASH_TK
    n_chunks = T // tk
    unroll = min(FLASH_MAX_UNROLL, max(2, n_chunks // 2))
    assert unroll % 2 == 0 and n_chunks % unroll == 0
    width = C_GROUP * HEAD_DIM
    return pl.pallas_call(
        functools.partial(_flash_kernel, tq=tq, tk=tk, n_chunks=n_chunks, unroll=unroll),
        grid=(n, C_KV_HEADS, T // tq),
        in_specs=[
            pl.BlockSpec(memory_space=pltpu.SMEM),
            pl.BlockSpec((None, width, tq), lambda b, g, i: (b, g, i)),
            pl.BlockSpec((None, None, T, K_COLS), lambda b, g, i: (b, g, 0, 0)),
            pl.BlockSpec((None, n_chunks, V_ROWS, tk), lambda b, g, i: (b, 0, g, 0)),
        ],
        out_specs=pl.BlockSpec((None, tq, width), lambda b, g, i: (b, i, g)),
        out_shape=jax.ShapeDtypeStruct((n, T, C_Q), BF16),
        scratch_shapes=[
            pltpu.VMEM((C_GROUP, tk, tq), BF16), pltpu.VMEM((C_GROUP, tk, tq), BF16),
            pltpu.VMEM((C_GROUP, K_COLS, tq), BF16),
            pltpu.VMEM((C_GROUP, 1, tq), F32),
            pltpu.VMEM((C_GROUP, V_ROWS, tq), F32),
        ],
        compiler_params=_params(("parallel", "parallel", "parallel")),
        name="flash_c",
    )(bound, qt, k, vt)


def _out_kernel(x_ref, o_ref, wo_ref, g_ref, w1_ref, w3_ref, w2_ref, gf_ref, out_ref):
    x1 = x_ref[...] + jnp.dot(o_ref[...], wo_ref[...], preferred_element_type=F32)
    out_ref[...] = _rms(_ffn(x1, g_ref, w1_ref, w3_ref, w2_ref), gf_ref[...])


def _out(x, o, wo, g, w1, w3, w2, gf):
    n, T, _ = x.shape
    tm = ROW_TILE
    tok = pl.BlockSpec((None, tm, D_MODEL), lambda b, i: (b, i, 0))
    return pl.pallas_call(
        _out_kernel,
        grid=(n, T // tm),
        in_specs=[tok, tok, _const_spec(wo.shape), _const_spec((1, D_MODEL)),
                  _const_spec(w1.shape), _const_spec(w3.shape), _const_spec(w2.shape),
                  _const_spec((1, D_MODEL))],
        out_specs=tok,
        out_shape=jax.ShapeDtypeStruct((n, T, D_MODEL), F32),
        compiler_params=_params(("parallel", "parallel")),
        name="out_ffn_norm",
    )(x, o, wo, g, w1, w3, w2, gf)


def _rope_tables(T):
    n_freq = HEAD_DIM // 4
    inv = ROPE_THETA ** (-jnp.arange(n_freq, dtype=F32) / n_freq)
    rows = T // GRID_W
    row = jnp.repeat(jnp.arange(rows, dtype=F32), GRID_W)
    col = jnp.tile(jnp.arange(GRID_W, dtype=F32), rows)
    ang = jnp.concatenate([row[:, None] * inv, col[:, None] * inv], axis=-1)
    return jnp.cos(ang).T, jnp.sin(ang).T


def _prepare(norm_mix, w_in_ab, w_out_ab, sink_a, w_in_c, w_out_c, q_gain_c, k_gain_c,
             norm_ffn, ffn_w1, ffn_w3, ffn_w2, final_norm):
    col_scale = np.ones((1, PA_WIDTH + 2 * GROUP_WIDTH), np.float32)
    col_scale[:, :A_Q] = QK_SCALE * LOG2E
    for gidx in range(len(B_PATTERNS)):
        base = A_Q + 2 * A_KV + gidx * GROUP_WIDTH
        col_scale[:, base:base + B_WIDTH] = QK_SCALE * LOG2E
    pair_perm = np.concatenate([np.arange(0, HEAD_DIM, 2), np.arange(1, HEAD_DIM, 2)])
    row_perm = np.concatenate(
        [h * HEAD_DIM + pair_perm for h in range(C_HEADS + C_KV_HEADS)]
        + [np.arange(C_Q + C_KV, C_IN)])
    row = lambda v: v.reshape(1, D_MODEL).astype(F32)
    gain = lambda v: jnp.broadcast_to(v.astype(F32)[pair_perm][:, None], (HEAD_DIM, ROW_TILE))
    return dict(
        g_mix0=row(norm_mix[0]), g_mix1=row(norm_mix[1]),
        g_ffn0=row(norm_ffn[0]), g_ffn1=row(norm_ffn[1]), g_final=row(final_norm),
        w_ab=(w_in_ab[0] * col_scale).astype(BF16),
        wo_ab=w_out_ab[0].astype(BF16),
        sink=sink_a[0].reshape(1, A_HEADS).astype(F32),
        wt_c=w_in_c[0].T[row_perm].astype(BF16),
        wo_c=w_out_c[0].astype(BF16),
        gq=gain(q_gain_c[0]), gk=gain(k_gain_c[0]),
        score_bound=(1.02 * HEAD_DIM * QK_SCALE * LOG2E * jnp.max(jnp.abs(q_gain_c[0]))
                     * jnp.max(jnp.abs(k_gain_c[0]))).astype(F32).reshape(1, 1),
        w1=[ffn_w1[l].astype(BF16) for l in range(2)],
        w3=[ffn_w3[l].astype(BF16) for l in range(2)],
        w2=[ffn_w2[l].astype(BF16) for l in range(2)],
    )


def _trunk(x, p):
    n, T, _ = x.shape
    pa, pb1, pb2 = _proj_ab(x, p["g_mix0"], p["w_ab"])
    slopes_a = _alibi_slopes(A_HEADS)
    (oa,) = _band_attention(
        pa, q_col=0, k_col=A_Q // A_KV, v_col=A_Q // A_KV + 1, q_width=A_Q, kv_width=A_KV,
        radius=A_WINDOW, slopes=slopes_a, sink=p["sink"], emit_lse=False)
    slopes_b = _alibi_slopes(B_HEADS).reshape(len(B_PATTERNS), B_HEADS_PER_GROUP)
    ob, lb = [], []
    for gidx, (src, first) in enumerate((
            (pa, (A_Q + 2 * A_KV) // B_WIDTH),
            (pb1.reshape(n * B_PATTERNS[1][1], T // B_PATTERNS[1][1], GROUP_WIDTH), 0),
            (pb2.reshape(n * B_PATTERNS[2][1], T // B_PATTERNS[2][1], GROUP_WIDTH), 0))):
        dil = B_PATTERNS[gidx][1]
        o, lse = _band_attention(
            src, q_col=first, k_col=first + 1, v_col=first + 2, q_width=B_WIDTH,
            kv_width=B_WIDTH, radius=B_RADIUS, slopes=slopes_b[gidx] * np.float32(dil),
            emit_lse=True)
        ob.append(o.reshape(n, dil, T // dil, B_WIDTH) if dil > 1 else o)
        lb.append(lse.reshape(n, dil, T // dil, B_WIDTH) if dil > 1 else lse)
    x = _mid(x, oa, ob[0], lb[0], ob[1], lb[1], ob[2], lb[2], p["wo_ab"], p["g_ffn0"],
             p["w1"][0], p["w3"][0], p["w2"][0])
    cos, sin = _rope_tables(T)
    qt, k, vt = _proj_c(x, p["g_mix1"], p["wt_c"], p["gq"], p["gk"],
                        cos * (QK_SCALE * LOG2E), sin * (QK_SCALE * LOG2E), cos, sin)
    o = _flash(qt, k, vt, p["score_bound"])
    return _out(x, o, p["wo_c"], p["g_ffn1"], p["w1"][1], p["w3"][1], p["w2"][1], p["g_final"])


def kernel(x_prompt, x_sample, norm_mix, w_in_ab, w_out_ab, sink_a, w_in_c, w_out_c, q_gain_c,
           k_gain_c, norm_ffn, ffn_w1, ffn_w3, ffn_w2, final_norm):
    assert norm_mix.shape[0] == 2 and all(window // (2 * dil) == B_RADIUS
                                          for window, dil in B_PATTERNS)
    p = _prepare(norm_mix, w_in_ab, w_out_ab, sink_a, w_in_c, w_out_c, q_gain_c, k_gain_c,
                 norm_ffn, ffn_w1, ffn_w3, ffn_w2, final_norm)
    return _trunk(x_prompt, p), _trunk(x_sample, p)
```

```python
import functools

import numpy as np
import jax
import jax.numpy as jnp
from jax import lax
from jax.experimental import pallas as pl
from jax.experimental.pallas import tpu as pltpu

F32 = jnp.float32
BF16 = jnp.bfloat16

D_MODEL = 1024
HEAD_DIM = 64
EPS = 1e-6
NEG = -1e30
A_HEADS = 8
A_KV_HEADS = 2
A_WINDOW = 128
B_PATTERNS = ((128, 1), (512, 4), (2048, 16))
B_HEADS_PER_GROUP = 4
B_HEADS = B_HEADS_PER_GROUP * len(B_PATTERNS)
B_RADIUS = 64
B_WIDTH = B_HEADS_PER_GROUP * HEAD_DIM
C_HEADS = 16
C_KV_HEADS = 4
C_GROUP = C_HEADS // C_KV_HEADS
ROPE_THETA = 10000.0
GRID_W = 64
D_FF = 2816
A_Q = A_HEADS * HEAD_DIM
A_KV = A_KV_HEADS * HEAD_DIM
PA_WIDTH = A_Q + 2 * A_KV + 3 * B_WIDTH
GROUP_WIDTH = 3 * B_WIDTH
AB_OUT = A_Q + B_WIDTH
C_Q = C_HEADS * HEAD_DIM
C_KV = C_KV_HEADS * HEAD_DIM
C_IN = C_Q + 2 * C_KV
QK_SCALE = HEAD_DIM ** -0.5
LOG2E = 1.4426950408889634
BF16_TILE_ROWS = 16
V_ROWS = HEAD_DIM + BF16_TILE_ROWS
REF_ROWS = HEAD_DIM + BF16_TILE_ROWS
K_COLS = 128
UNDERFLOW_GUARD = 60.0

LANES = 128
ROW_TILE = 512
BAND_TQ = 256
BAND_SEQS = 8
FLASH_TQ = 512
FLASH_TK = ROW_TILE
FLASH_MAX_UNROLL = 8
FF_CHUNKS = ((0, 1536), (1536, D_FF))
VMEM_LIMIT = 56 * 1024 * 1024


def _alibi_slopes(n):
    return np.asarray(2.0 ** (-8.0 * np.arange(1, n + 1) / n), dtype=np.float32)


def _const_spec(shape):
    zeros = (0,) * len(shape)
    return pl.BlockSpec(shape, lambda *_: zeros, pipeline_mode=pl.Buffered(1))


def _params(sem):
    return pltpu.CompilerParams(dimension_semantics=sem, vmem_limit_bytes=VMEM_LIMIT)


def _rms(x, g):
    return x * lax.rsqrt(jnp.mean(x * x, axis=-1, keepdims=True) + EPS) * g


def _ffn(x1, g_ref, w1_ref, w3_ref, w2_ref):
    h = _rms(x1, g_ref[...]).astype(BF16)
    acc = None
    for c0, c1 in FF_CHUNKS:
        u = jnp.dot(h, w1_ref[:, c0:c1], preferred_element_type=F32)
        v = jnp.dot(h, w3_ref[:, c0:c1], preferred_element_type=F32)
        t = (u * (1.0 / (1.0 + jnp.exp(-u))) * v).astype(BF16)
        part = jnp.dot(t, w2_ref[c0:c1, :], preferred_element_type=F32)
        acc = part if acc is None else acc + part
    return x1 + acc


def _proj_ab_kernel(x_ref, g_ref, w_ref, pa_ref, pb1_ref, pb2_ref, slab_ref, *, tm):
    h = _rms(x_ref[...], g_ref[...]).astype(BF16)
    pa_ref[...] = jnp.dot(h, w_ref[:, :PA_WIDTH], preferred_element_type=F32).astype(BF16)
    n_slab = GROUP_WIDTH // LANES
    for (_, dil), out_ref, c0 in ((B_PATTERNS[1], pb1_ref, PA_WIDTH),
                                  (B_PATTERNS[2], pb2_ref, PA_WIDTH + GROUP_WIDTH)):
        r = jnp.dot(h, w_ref[:, c0:c0 + GROUP_WIDTH], preferred_element_type=F32)
        for s in range(n_slab):
            slab_ref[s] = r[:, s * LANES:(s + 1) * LANES]
        rows = tm // dil
        for d in range(dil):
            for s in range(n_slab):
                out_ref[d, :, s * LANES:(s + 1) * LANES] = (
                    slab_ref[s, pl.ds(d, rows, stride=dil), :].astype(BF16))


def _proj_ab(x, g, w):
    n, T, _ = x.shape
    tm = ROW_TILE
    d1, d2 = B_PATTERNS[1][1], B_PATTERNS[2][1]
    return pl.pallas_call(
        functools.partial(_proj_ab_kernel, tm=tm),
        grid=(n, T // tm),
        in_specs=[
            pl.BlockSpec((None, tm, D_MODEL), lambda b, i: (b, i, 0)),
            _const_spec((1, D_MODEL)),
            _const_spec(w.shape),
        ],
        out_specs=[
            pl.BlockSpec((None, tm, PA_WIDTH), lambda b, i: (b, i, 0)),
            pl.BlockSpec((None, d1, tm // d1, GROUP_WIDTH), lambda b, i: (b, 0, i, 0)),
            pl.BlockSpec((None, d2, tm // d2, GROUP_WIDTH), lambda b, i: (b, 0, i, 0)),
        ],
        out_shape=[
            jax.ShapeDtypeStruct((n, T, PA_WIDTH), BF16),
            jax.ShapeDtypeStruct((n, d1, T // d1, GROUP_WIDTH), BF16),
            jax.ShapeDtypeStruct((n, d2, T // d2, GROUP_WIDTH), BF16),
        ],
        scratch_shapes=[pltpu.VMEM((GROUP_WIDTH // LANES, tm, LANES), F32)],
        compiler_params=_params(("parallel", "parallel")),
        name="proj_ab",
    )(x, g, w)


def _band_kernel(*refs, n_seq, n_q, n_kv, has_sink, emit_lse):
    q_ref, kp_ref, kc_ref, kn_ref, vp_ref, vc_ref, vn_ref, bias_ref = refs[:8]
    pos = 8
    sink_ref = None
    if has_sink:
        sink_ref = refs[pos]
        pos += 1
    o_ref = refs[pos]
    pos += 1
    lse_ref = None
    if emit_lse:
        lse_ref = refs[pos]
        pos += 1
    s_scr, m_scr, ot_scr = refs[pos:pos + 3]
    lt_scr = refs[pos + 3] if emit_lse else None

    tq = q_ref.shape[1]
    width = kc_ref.shape[1] + 2 * kp_ref.shape[1]
    group = n_q // n_kv
    pad = V_ROWS - HEAD_DIM
    ones_row = (lax.broadcasted_iota(jnp.int32, (pad, width), 0) == 0).astype(BF16)
    ks, vts = [], []
    for b in range(n_seq):
        k_all = jnp.concatenate([kp_ref[b], kc_ref[b], kn_ref[b]], axis=0)
        v_all = jnp.concatenate([vp_ref[b], vc_ref[b], vn_ref[b]], axis=0)
        vt = v_all.astype(F32).T
        ks.append([k_all[:, j * HEAD_DIM:(j + 1) * HEAD_DIM] for j in range(n_kv)])
        vts.append([jnp.concatenate([vt[j * HEAD_DIM:(j + 1) * HEAD_DIM].astype(BF16), ones_row],
                                    axis=0) for j in range(n_kv)])
    sinks = [sink_ref[0, h] * LOG2E for h in range(n_q)] if has_sink else None
    units = [(b, h) for b in range(n_seq) for h in range(n_q)]

    def score_phase(u):
        b, h = units[u]
        qh = q_ref[b, :, h * HEAD_DIM:(h + 1) * HEAD_DIM]
        s = lax.dot_general(ks[b][h // group], qh, (((1,), (1,)), ((), ())),
                            preferred_element_type=F32) + bias_ref[h]
        m = jnp.max(s, axis=0, keepdims=True)
        if has_sink:
            m = jnp.maximum(m, sinks[h])
        s_scr[u % 2] = s
        m_scr[u % 2] = m

    def value_phase(u):
        b, h = units[u]
        rows = slice(h * HEAD_DIM, (h + 1) * HEAD_DIM)
        m = m_scr[u % 2]
        p = jnp.exp2(s_scr[u % 2] - m).astype(BF16)
        oe = jnp.dot(vts[b][h // group], p, preferred_element_type=F32)
        den = oe[HEAD_DIM:HEAD_DIM + 1]
        if has_sink:
            den = den + jnp.exp2(sinks[h] - m)
        ot_scr[b, rows, :] = oe[:HEAD_DIM] / den
        if emit_lse:
            lt_scr[b, rows, :] = jnp.broadcast_to(m + jnp.log2(den), (HEAD_DIM, tq))

    score_phase(0)
    for u in range(len(units)):
        if u + 1 < len(units):
            score_phase(u + 1)
        value_phase(u)
    for b in range(n_seq):
        o_ref[b] = ot_scr[b].T.astype(o_ref.dtype)
        if emit_lse:
            lse_ref[b] = lt_scr[b].T


def _band_bias(tq, radius, slopes):
    width = tq + 2 * radius
    col = lax.broadcasted_iota(jnp.int32, (width, tq), 0)
    row = lax.broadcasted_iota(jnp.int32, (width, tq), 1)
    dist = jnp.abs(col - radius - row)
    slope2 = jnp.asarray(np.asarray(slopes, np.float32) * np.float32(LOG2E))
    alibi = -slope2[:, None, None] * dist.astype(F32)[None]
    tables = []
    for cls in range(4):
        ok = dist <= radius
        if cls & 1:
            ok = ok & (col >= radius)
        if cls & 2:
            ok = ok & (col < radius + tq)
        tables.append(jnp.where(ok[None], alibi, NEG))
    return jnp.stack(tables, axis=0)


def _band_attention(src, *, q_col, k_col, v_col, q_width, kv_width, radius, slopes,
                    sink=None, emit_lse):
    nb, seq, _ = src.shape
    tq = min(BAND_TQ, seq)
    rb = tq // radius
    n_blk = seq // tq
    last = seq // radius - 1
    n_q = q_width // HEAD_DIM
    n_kv = kv_width // HEAD_DIM
    width = tq + 2 * radius

    n_seq = min(BAND_SEQS, nb)
    assert nb % n_seq == 0

    def cur(col):
        return lambda i, b: (b, i, col)

    def prev(col):
        return lambda i, b: (b, jnp.maximum(i * rb - 1, 0), col)

    def nxt(col):
        return lambda i, b: (b, jnp.minimum((i + 1) * rb, last), col)

    def edge_class(i, b):
        cls = (i == 0).astype(jnp.int32) + 2 * (i == n_blk - 1).astype(jnp.int32)
        return (cls, 0, 0, 0)

    in_specs = [pl.BlockSpec((n_seq, tq, q_width), cur(q_col))]
    for col in (k_col, v_col):
        in_specs += [
            pl.BlockSpec((n_seq, radius, kv_width), prev(col)),
            pl.BlockSpec((n_seq, tq, kv_width), cur(col)),
            pl.BlockSpec((n_seq, radius, kv_width), nxt(col)),
        ]
    in_specs.append(pl.BlockSpec((None, n_q, width, tq), edge_class))
    args = [src] * 7 + [_band_bias(tq, radius, slopes)]
    if sink is not None:
        in_specs.append(pl.BlockSpec(memory_space=pltpu.SMEM))
        args.append(sink)
    out_dtype = F32 if emit_lse else BF16
    out_specs = [pl.BlockSpec((n_seq, tq, q_width), lambda i, b: (b, i, 0))]
    out_shape = [jax.ShapeDtypeStruct((nb, seq, q_width), out_dtype)]
    if emit_lse:
        out_specs.append(pl.BlockSpec((n_seq, tq, q_width), lambda i, b: (b, i, 0)))
        out_shape.append(jax.ShapeDtypeStruct((nb, seq, q_width), F32))
    kernel = functools.partial(
        _band_kernel, n_seq=n_seq, n_q=n_q, n_kv=n_kv, has_sink=sink is not None,
        emit_lse=emit_lse)
    return pl.pallas_call(
        kernel,
        grid=(n_blk, nb // n_seq),
        in_specs=in_specs,
        out_specs=out_specs,
        out_shape=out_shape,
        scratch_shapes=[pltpu.VMEM((2, width, tq), F32), pltpu.VMEM((2, 1, tq), F32)]
        + [pltpu.VMEM((n_seq, q_width, tq), F32)] * (2 if emit_lse else 1),
        compiler_params=_params(("parallel", "parallel")),
        name="band_attention",
    )(*args)


def _mid_kernel(x_ref, oa_ref, o0_ref, l0_ref, o1_ref, l1_ref, o2_ref, l2_ref, wo_ref,
                g_ref, w1_ref, w3_ref, w2_ref, out_ref, slab_ref, *, tm):
    n_slab = B_WIDTH // LANES
    for base, src, (_, dil) in ((0, o1_ref, B_PATTERNS[1]), (n_slab, l1_ref, B_PATTERNS[1]),
                                (2 * n_slab, o2_ref, B_PATTERNS[2]),
                                (3 * n_slab, l2_ref, B_PATTERNS[2])):
        rows = tm // dil
        for d in range(dil):
            for s in range(n_slab):
                slab_ref[base + s, pl.ds(d, rows, stride=dil), :] = (
                    src[d, :, s * LANES:(s + 1) * LANES])
    mixed = []
    for s in range(n_slab):
        sl = slice(s * LANES, (s + 1) * LANES)
        l0, l1, l2 = l0_ref[:, sl], slab_ref[n_slab + s], slab_ref[3 * n_slab + s]
        mx = jnp.maximum(jnp.maximum(l0, l1), l2)
        e0, e1, e2 = jnp.exp2(l0 - mx), jnp.exp2(l1 - mx), jnp.exp2(l2 - mx)
        num = e0 * o0_ref[:, sl] + e1 * slab_ref[s] + e2 * slab_ref[2 * n_slab + s]
        mixed.append(num / (e0 + e1 + e2))
    ob = jnp.concatenate(mixed, axis=-1).astype(BF16)
    attn = (jnp.dot(oa_ref[...], wo_ref[:A_Q, :], preferred_element_type=F32)
            + jnp.dot(ob, wo_ref[A_Q:, :], preferred_element_type=F32))
    out_ref[...] = _ffn(x_ref[...] + attn, g_ref, w1_ref, w3_ref, w2_ref)


def _mid(x, oa, o0, l0, o1, l1, o2, l2, wo, g, w1, w3, w2):
    n, T, _ = x.shape
    tm = ROW_TILE
    d1, d2 = B_PATTERNS[1][1], B_PATTERNS[2][1]
    tok = lambda width: pl.BlockSpec((None, tm, width), lambda b, i: (b, i, 0))
    res = lambda dil: pl.BlockSpec((None, dil, tm // dil, B_WIDTH), lambda b, i: (b, 0, i, 0))
    return pl.pallas_call(
        functools.partial(_mid_kernel, tm=tm),
        grid=(n, T // tm),
        in_specs=[tok(D_MODEL), tok(A_Q), tok(B_WIDTH), tok(B_WIDTH), res(d1), res(d1),
                  res(d2), res(d2), _const_spec(wo.shape), _const_spec((1, D_MODEL)),
                  _const_spec(w1.shape), _const_spec(w3.shape), _const_spec(w2.shape)],
        out_specs=tok(D_MODEL),
        out_shape=jax.ShapeDtypeStruct((n, T, D_MODEL), F32),
        scratch_shapes=[pltpu.VMEM((4 * B_WIDTH // LANES, tm, LANES), F32)],
        compiler_params=_params(("parallel", "parallel")),
        name="mix_out_ffn",
    )(x, oa, o0, l0, o1, l1, o2, l2, wo, g, w1, w3, w2)


def _norm_rope(xh, gain, cos, sin):
    half = HEAD_DIM // 2
    y = xh * lax.rsqrt(jnp.mean(xh * xh, axis=0, keepdims=True) + EPS) * gain
    x0, x1 = y[:half], y[half:]
    return jnp.concatenate([x0 * cos - x1 * sin, x0 * sin + x1 * cos], axis=0)


def _proj_c_kernel(x_ref, g_ref, w_ref, gq_ref, gk_ref, cq_ref, sq_ref, ck_ref, sk_ref,
                   qt_ref, k_ref, vt_ref):
    h = _rms(x_ref[...], g_ref[...]).astype(BF16)
    pt = lax.dot_general(w_ref[...], h, (((1,), (1,)), ((), ())), preferred_element_type=F32)
    gq, gk = gq_ref[...], gk_ref[...]
    cq, sq, ck, sk = cq_ref[...], sq_ref[...], ck_ref[...], sk_ref[...]
    for hd in range(C_HEADS):
        rows = slice(hd * HEAD_DIM, (hd + 1) * HEAD_DIM)
        qt_ref[rows, :] = _norm_rope(pt[rows], gq, cq, sq).astype(BF16)
    one_feature = (lax.broadcasted_iota(jnp.int32, (K_COLS - HEAD_DIM, pt.shape[1]), 0) == 0
                   ).astype(F32)
    pieces = []
    for j in range(C_KV_HEADS):
        pieces += [_norm_rope(pt[C_Q + j * HEAD_DIM:C_Q + (j + 1) * HEAD_DIM], gk, ck, sk),
                   one_feature]
    k_nat = jnp.concatenate(pieces, axis=0).T
    for j in range(C_KV_HEADS):
        k_ref[j] = k_nat[:, j * K_COLS:(j + 1) * K_COLS].astype(BF16)
    vt_ref[...] = pt[C_Q + C_KV:].astype(BF16)


def _proj_c(x, g, wt, gq, gk, cq, sq, ck, sk):
    n, T, _ = x.shape
    tm = ROW_TILE
    half = HEAD_DIM // 2
    tab = pl.BlockSpec((half, tm), lambda b, i: (0, i))
    return pl.pallas_call(
        _proj_c_kernel,
        grid=(n, T // tm),
        in_specs=[
            pl.BlockSpec((None, tm, D_MODEL), lambda b, i: (b, i, 0)),
            _const_spec((1, D_MODEL)),
            _const_spec(wt.shape),
            _const_spec((HEAD_DIM, tm)),
            _const_spec((HEAD_DIM, tm)),
            tab, tab, tab, tab,
        ],
        out_specs=[
            pl.BlockSpec((None, C_Q, tm), lambda b, i: (b, 0, i)),
            pl.BlockSpec((None, C_KV_HEADS, tm, K_COLS), lambda b, i: (b, 0, i, 0)),
            pl.BlockSpec((None, None, C_KV, tm), lambda b, i: (b, i, 0, 0)),
        ],
        out_shape=[
            jax.ShapeDtypeStruct((n, C_Q, T), BF16),
            jax.ShapeDtypeStruct((n, C_KV_HEADS, T, K_COLS), BF16),
            jax.ShapeDtypeStruct((n, T // tm, C_KV, tm), BF16),
        ],
        compiler_params=_params(("parallel", "parallel")),
        name="proj_c",
    )(x, g, wt, gq, gk, cq, sq, ck, sk)


def _flash_kernel(bound_ref, qt_ref, k_ref, vt_ref, o_ref, sa_ref, sb_ref, qx_ref, mt_ref,
                  l_ref, acc_ref, *, tq, tk, n_chunks, unroll):
    ref_tile = REF_ROWS - HEAD_DIM
    first_row = lax.broadcasted_iota(jnp.int32, (ref_tile, tq), 0) == 0
    for hh in range(C_GROUP):
        qx_ref[hh, :HEAD_DIM, :] = qt_ref[hh * HEAD_DIM:(hh + 1) * HEAD_DIM, :]
        qx_ref[hh, REF_ROWS:, :] = jnp.zeros((K_COLS - REF_ROWS, tq), BF16)
    mt_ref[...] = jnp.zeros_like(mt_ref)
    bound = jnp.full((1, tq), bound_ref[0, 0], F32).astype(BF16).astype(F32)

    def scores(j, p_ref):
        kb = k_ref[pl.ds(pl.multiple_of(j * tk, tk), tk), :]
        for hh in range(C_GROUP):
            s = jnp.dot(kb, qx_ref[hh], preferred_element_type=F32)
            p = jnp.exp2(s)
            p_ref[hh] = p.astype(BF16)
            l_ref[hh] += jnp.sum(p, axis=0, keepdims=True)
            mt_ref[hh] = jnp.maximum(mt_ref[hh], jnp.max(s, axis=0, keepdims=True))

    def accumulate(j, p_ref):
        vb = vt_ref[j]
        for hh in range(C_GROUP):
            acc_ref[hh] += jnp.dot(vb, p_ref[hh], preferred_element_type=F32)

    bufs = (sa_ref, sb_ref)

    def stage(j, parity):
        scores(j + 1, bufs[1 - parity])
        accumulate(j, bufs[parity])

    def trip(t, carry):
        for u in range(unroll):
            stage(t * unroll + u, u % 2)
        return carry

    def one_pass(state):
        n_done, _ = state
        worst = None
        for hh in range(C_GROUP):
            ref = jnp.where(n_done == 0, bound, bound + mt_ref[hh])
            qx_ref[hh, HEAD_DIM:REF_ROWS, :] = jnp.where(first_row, -ref, 0.0).astype(BF16)
        acc_ref[...] = jnp.zeros_like(acc_ref)
        l_ref[...] = jnp.zeros_like(l_ref)
        mt_ref[...] = jnp.full_like(mt_ref, NEG)
        scores(0, bufs[0])
        lax.fori_loop(0, n_chunks // unroll - 1, trip, 0)
        for j in range(n_chunks - unroll, n_chunks - 1):
            stage(j, j % 2)
        accumulate(n_chunks - 1, bufs[(n_chunks - 1) % 2])
        for hh in range(C_GROUP):
            low = jnp.min(mt_ref[hh])
            worst = low if worst is None else jnp.minimum(worst, low)
        again = jnp.logical_and(n_done == 0, worst < -UNDERFLOW_GUARD)
        return n_done + 1, again.astype(jnp.int32)

    lax.while_loop(lambda state: state[1] != 0, one_pass, (jnp.int32(0), jnp.int32(1)))
    ot = jnp.concatenate([acc_ref[hh] / l_ref[hh] for hh in range(C_GROUP)], axis=0)
    o_ref[...] = ot.T.astype(BF16)


def _flash(qt, k, vt, bound):
    n, _, T = qt.shape
    tq, tk = FLASH_TQ, FLASH_TK
    n_chunks = T // tk
    unroll = min(FLASH_MAX_UNROLL, max(2, n_chunks // 2))
    assert unroll % 2 == 0 and n_chunks % unroll == 0
    width = C_GROUP * HEAD_DIM
    return pl.pallas_call(
        functools.partial(_flash_kernel, tq=tq, tk=tk, n_chunks=n_chunks, unroll=unroll),
        grid=(n, C_KV_HEADS, T // tq),
        in_specs=[
            pl.BlockSpec(memory_space=pltpu.SMEM),
            pl.BlockSpec((None, width, tq), lambda b, g, i: (b, g, i)),
            pl.BlockSpec((None, None, T, K_COLS), lambda b, g, i: (b, g, 0, 0)),
            pl.BlockSpec((None, n_chunks, HEAD_DIM, tk), lambda b, g, i: (b, 0, g, 0)),
        ],
        out_specs=pl.BlockSpec((None, tq, width), lambda b, g, i: (b, i, g)),
        out_shape=jax.ShapeDtypeStruct((n, T, C_Q), BF16),
        scratch_shapes=[
            pltpu.VMEM((C_GROUP, tk, tq), BF16), pltpu.VMEM((C_GROUP, tk, tq), BF16),
            pltpu.VMEM((C_GROUP, K_COLS, tq), BF16),
            pltpu.VMEM((C_GROUP, 1, tq), F32),
            pltpu.VMEM((C_GROUP, 1, tq), F32),
            pltpu.VMEM((C_GROUP, HEAD_DIM, tq), F32),
        ],
        compiler_params=_params(("parallel", "parallel", "parallel")),
        name="flash_c",
    )(bound, qt, k, vt)


def _out_kernel(x_ref, o_ref, wo_ref, g_ref, w1_ref, w3_ref, w2_ref, gf_ref, out_ref):
    x1 = x_ref[...] + jnp.dot(o_ref[...], wo_ref[...], preferred_element_type=F32)
    out_ref[...] = _rms(_ffn(x1, g_ref, w1_ref, w3_ref, w2_ref), gf_ref[...])


def _out(x, o, wo, g, w1, w3, w2, gf):
    n, T, _ = x.shape
    tm = ROW_TILE
    tok = pl.BlockSpec((None, tm, D_MODEL), lambda b, i: (b, i, 0))
    return pl.pallas_call(
        _out_kernel,
        grid=(n, T // tm),
        in_specs=[tok, tok, _const_spec(wo.shape), _const_spec((1, D_MODEL)),
                  _const_spec(w1.shape), _const_spec(w3.shape), _const_spec(w2.shape),
                  _const_spec((1, D_MODEL))],
        out_specs=tok,
        out_shape=jax.ShapeDtypeStruct((n, T, D_MODEL), F32),
        compiler_params=_params(("parallel", "parallel")),
        name="out_ffn_norm",
    )(x, o, wo, g, w1, w3, w2, gf)


def _rope_tables(T):
    n_freq = HEAD_DIM // 4
    inv = ROPE_THETA ** (-jnp.arange(n_freq, dtype=F32) / n_freq)
    rows = T // GRID_W
    row = jnp.repeat(jnp.arange(rows, dtype=F32), GRID_W)
    col = jnp.tile(jnp.arange(GRID_W, dtype=F32), rows)
    ang = jnp.concatenate([row[:, None] * inv, col[:, None] * inv], axis=-1)
    return jnp.cos(ang).T, jnp.sin(ang).T


def _prepare(norm_mix, w_in_ab, w_out_ab, sink_a, w_in_c, w_out_c, q_gain_c, k_gain_c,
             norm_ffn, ffn_w1, ffn_w3, ffn_w2, final_norm):
    col_scale = np.ones((1, PA_WIDTH + 2 * GROUP_WIDTH), np.float32)
    col_scale[:, :A_Q] = QK_SCALE * LOG2E
    for gidx in range(len(B_PATTERNS)):
        base = A_Q + 2 * A_KV + gidx * GROUP_WIDTH
        col_scale[:, base:base + B_WIDTH] = QK_SCALE * LOG2E
    pair_perm = np.concatenate([np.arange(0, HEAD_DIM, 2), np.arange(1, HEAD_DIM, 2)])
    row_perm = np.concatenate(
        [h * HEAD_DIM + pair_perm for h in range(C_HEADS + C_KV_HEADS)]
        + [np.arange(C_Q + C_KV, C_IN)])
    row = lambda v: v.reshape(1, D_MODEL).astype(F32)
    gain = lambda v: jnp.broadcast_to(v.astype(F32)[pair_perm][:, None], (HEAD_DIM, ROW_TILE))
    return dict(
        g_mix0=row(norm_mix[0]), g_mix1=row(norm_mix[1]),
        g_ffn0=row(norm_ffn[0]), g_ffn1=row(norm_ffn[1]), g_final=row(final_norm),
        w_ab=(w_in_ab[0] * col_scale).astype(BF16),
        wo_ab=w_out_ab[0].astype(BF16),
        sink=sink_a[0].reshape(1, A_HEADS).astype(F32),
        wt_c=w_in_c[0].T[row_perm].astype(BF16),
        wo_c=w_out_c[0].astype(BF16),
        gq=gain(q_gain_c[0]), gk=gain(k_gain_c[0]),
        score_bound=(1.02 * HEAD_DIM * QK_SCALE * LOG2E * jnp.max(jnp.abs(q_gain_c[0]))
                     * jnp.max(jnp.abs(k_gain_c[0]))).astype(F32).reshape(1, 1),
        w1=[ffn_w1[l].astype(BF16) for l in range(2)],
        w3=[ffn_w3[l].astype(BF16) for l in range(2)],
        w2=[ffn_w2[l].astype(BF16) for l in range(2)],
    )


def _trunk(x, p):
    n, T, _ = x.shape
    pa, pb1, pb2 = _proj_ab(x, p["g_mix0"], p["w_ab"])
    slopes_a = _alibi_slopes(A_HEADS)
    (oa,) = _band_attention(
        pa, q_col=0, k_col=A_Q // A_KV, v_col=A_Q // A_KV + 1, q_width=A_Q, kv_width=A_KV,
        radius=A_WINDOW, slopes=slopes_a, sink=p["sink"], emit_lse=False)
    slopes_b = _alibi_slopes(B_HEADS).reshape(len(B_PATTERNS), B_HEADS_PER_GROUP)
    ob, lb = [], []
    for gidx, (src, first) in enumerate((
            (pa, (A_Q + 2 * A_KV) // B_WIDTH),
            (pb1.reshape(n * B_PATTERNS[1][1], T // B_PATTERNS[1][1], GROUP_WIDTH), 0),
            (pb2.reshape(n * B_PATTERNS[2][1], T // B_PATTERNS[2][1], GROUP_WIDTH), 0))):
        dil = B_PATTERNS[gidx][1]
        o, lse = _band_attention(
            src, q_col=first, k_col=first + 1, v_col=first + 2, q_width=B_WIDTH,
            kv_width=B_WIDTH, radius=B_RADIUS, slopes=slopes_b[gidx] * np.float32(dil),
            emit_lse=True)
        ob.append(o.reshape(n, dil, T // dil, B_WIDTH) if dil > 1 else o)
        lb.append(lse.reshape(n, dil, T // dil, B_WIDTH) if dil > 1 else lse)
    x = _mid(x, oa, ob[0], lb[0], ob[1], lb[1], ob[2], lb[2], p["wo_ab"], p["g_ffn0"],
             p["w1"][0], p["w3"][0], p["w2"][0])
    cos, sin = _rope_tables(T)
    qt, k, vt = _proj_c(x, p["g_mix1"], p["wt_c"], p["gq"], p["gk"],
                        cos * (QK_SCALE * LOG2E), sin * (QK_SCALE * LOG2E), cos, sin)
    o = _flash(qt, k, vt, p["score_bound"])
    return _out(x, o, p["wo_c"], p["g_ffn1"], p["w1"][1], p["w3"][1], p["w2"][1], p["g_final"])


def kernel(x_prompt, x_sample, norm_mix, w_in_ab, w_out_ab, sink_a, w_in_c, w_out_c, q_gain_c,
           k_gain_c, norm_ffn, ffn_w1, ffn_w3, ffn_w2, final_norm):
    assert norm_mix.shape[0] == 2 and all(window // (2 * dil) == B_RADIUS
                                          for window, dil in B_PATTERNS)
    p = _prepare(norm_mix, w_in_ab, w_out_ab, sink_a, w_in_c, w_out_c, q_gain_c, k_gain_c,
                 norm_ffn, ffn_w1, ffn_w3, ffn_w2, final_norm)
    return _trunk(x_prompt, p), _trunk(x_sample, p)
```

```python
import functools

import numpy as np
import jax
import jax.numpy as jnp
from jax import lax
from jax.experimental import pallas as pl
from jax.experimental.pallas import tpu as pltpu

F32 = jnp.float32
BF16 = jnp.bfloat16

D_MODEL = 1024
HEAD_DIM = 64
EPS = 1e-6
NEG = -1e30
A_HEADS = 8
A_KV_HEADS = 2
A_WINDOW = 128
B_PATTERNS = ((128, 1), (512, 4), (2048, 16))
B_HEADS_PER_GROUP = 4
B_HEADS = B_HEADS_PER_GROUP * len(B_PATTERNS)
B_RADIUS = 64
B_WIDTH = B_HEADS_PER_GROUP * HEAD_DIM
C_HEADS = 16
C_KV_HEADS = 4
C_GROUP = C_HEADS // C_KV_HEADS
ROPE_THETA = 10000.0
GRID_W = 64
D_FF = 2816
A_Q = A_HEADS * HEAD_DIM
A_KV = A_KV_HEADS * HEAD_DIM
PA_WIDTH = A_Q + 2 * A_KV + 3 * B_WIDTH
GROUP_WIDTH = 3 * B_WIDTH
AB_OUT = A_Q + B_WIDTH
C_Q = C_HEADS * HEAD_DIM
C_KV = C_KV_HEADS * HEAD_DIM
C_IN = C_Q + 2 * C_KV
QK_SCALE = HEAD_DIM ** -0.5
LOG2E = 1.4426950408889634
BF16_TILE_ROWS = 16
V_ROWS = HEAD_DIM + BF16_TILE_ROWS
REF_ROWS = HEAD_DIM + BF16_TILE_ROWS
K_COLS = 128
UNDERFLOW_GUARD = 60.0

LANES = 128
ROW_TILE = 512
BAND_TQ = 256
BAND_SEQS = 8
FLASH_TQ = 1024
FLASH_TK = ROW_TILE
FLASH_MAX_UNROLL = 8
FF_CHUNKS = ((0, 1536), (1536, D_FF))
VMEM_LIMIT = 56 * 1024 * 1024


def _alibi_slopes(n):
    return np.asarray(2.0 ** (-8.0 * np.arange(1, n + 1) / n), dtype=np.float32)


def _const_spec(shape):
    zeros = (0,) * len(shape)
    return pl.BlockSpec(shape, lambda *_: zeros, pipeline_mode=pl.Buffered(1))


def _params(sem):
    return pltpu.CompilerParams(dimension_semantics=sem, vmem_limit_bytes=VMEM_LIMIT)


def _rms(x, g):
    return x * lax.rsqrt(jnp.mean(x * x, axis=-1, keepdims=True) + EPS) * g


def _ffn(x1, g_ref, w1_ref, w3_ref, w2_ref):
    h = _rms(x1, g_ref[...]).astype(BF16)
    acc = None
    for c0, c1 in FF_CHUNKS:
        u = jnp.dot(h, w1_ref[:, c0:c1], preferred_element_type=F32)
        v = jnp.dot(h, w3_ref[:, c0:c1], preferred_element_type=F32)
        t = (u * (1.0 / (1.0 + jnp.exp(-u))) * v).astype(BF16)
        part = jnp.dot(t, w2_ref[c0:c1, :], preferred_element_type=F32)
        acc = part if acc is None else acc + part
    return x1 + acc


def _proj_ab_kernel(x_ref, g_ref, w_ref, pa_ref, pb1_ref, pb2_ref, slab_ref, *, tm):
    h = _rms(x_ref[...], g_ref[...]).astype(BF16)
    pa_ref[...] = jnp.dot(h, w_ref[:, :PA_WIDTH], preferred_element_type=F32).astype(BF16)
    n_slab = GROUP_WIDTH // LANES
    for (_, dil), out_ref, c0 in ((B_PATTERNS[1], pb1_ref, PA_WIDTH),
                                  (B_PATTERNS[2], pb2_ref, PA_WIDTH + GROUP_WIDTH)):
        r = jnp.dot(h, w_ref[:, c0:c0 + GROUP_WIDTH], preferred_element_type=F32)
        for s in range(n_slab):
            slab_ref[s] = r[:, s * LANES:(s + 1) * LANES]
        rows = tm // dil
        for d in range(dil):
            for s in range(n_slab):
                out_ref[d, :, s * LANES:(s + 1) * LANES] = (
                    slab_ref[s, pl.ds(d, rows, stride=dil), :].astype(BF16))


def _proj_ab(x, g, w):
    n, T, _ = x.shape
    tm = ROW_TILE
    d1, d2 = B_PATTERNS[1][1], B_PATTERNS[2][1]
    return pl.pallas_call(
        functools.partial(_proj_ab_kernel, tm=tm),
        grid=(n, T // tm),
        in_specs=[
            pl.BlockSpec((None, tm, D_MODEL), lambda b, i: (b, i, 0)),
            _const_spec((1, D_MODEL)),
            _const_spec(w.shape),
        ],
        out_specs=[
            pl.BlockSpec((None, tm, PA_WIDTH), lambda b, i: (b, i, 0)),
            pl.BlockSpec((None, d1, tm // d1, GROUP_WIDTH), lambda b, i: (b, 0, i, 0)),
            pl.BlockSpec((None, d2, tm // d2, GROUP_WIDTH), lambda b, i: (b, 0, i, 0)),
        ],
        out_shape=[
            jax.ShapeDtypeStruct((n, T, PA_WIDTH), BF16),
            jax.ShapeDtypeStruct((n, d1, T // d1, GROUP_WIDTH), BF16),
            jax.ShapeDtypeStruct((n, d2, T // d2, GROUP_WIDTH), BF16),
        ],
        scratch_shapes=[pltpu.VMEM((GROUP_WIDTH // LANES, tm, LANES), F32)],
        compiler_params=_params(("parallel", "parallel")),
        name="proj_ab",
    )(x, g, w)


def _band_kernel(*refs, n_seq, n_q, n_kv, has_sink, emit_lse):
    q_ref, kp_ref, kc_ref, kn_ref, vp_ref, vc_ref, vn_ref, bias_ref = refs[:8]
    pos = 8
    sink_ref = None
    if has_sink:
        sink_ref = refs[pos]
        pos += 1
    o_ref = refs[pos]
    pos += 1
    lse_ref = None
    if emit_lse:
        lse_ref = refs[pos]
        pos += 1
    s_scr, m_scr, ot_scr = refs[pos:pos + 3]
    lt_scr = refs[pos + 3] if emit_lse else None

    tq = q_ref.shape[1]
    width = kc_ref.shape[1] + 2 * kp_ref.shape[1]
    group = n_q // n_kv
    pad = V_ROWS - HEAD_DIM
    ones_row = (lax.broadcasted_iota(jnp.int32, (pad, width), 0) == 0).astype(BF16)
    ks, vts = [], []
    for b in range(n_seq):
        k_all = jnp.concatenate([kp_ref[b], kc_ref[b], kn_ref[b]], axis=0)
        v_all = jnp.concatenate([vp_ref[b], vc_ref[b], vn_ref[b]], axis=0)
        vt = v_all.astype(F32).T
        ks.append([k_all[:, j * HEAD_DIM:(j + 1) * HEAD_DIM] for j in range(n_kv)])
        vts.append([jnp.concatenate([vt[j * HEAD_DIM:(j + 1) * HEAD_DIM].astype(BF16), ones_row],
                                    axis=0) for j in range(n_kv)])
    sinks = [sink_ref[0, h] * LOG2E for h in range(n_q)] if has_sink else None
    units = [(b, h) for b in range(n_seq) for h in range(n_q)]

    def score_phase(u):
        b, h = units[u]
        qh = q_ref[b, :, h * HEAD_DIM:(h + 1) * HEAD_DIM]
        s = lax.dot_general(ks[b][h // group], qh, (((1,), (1,)), ((), ())),
                            preferred_element_type=F32) + bias_ref[h]
        m = jnp.max(s, axis=0, keepdims=True)
        if has_sink:
            m = jnp.maximum(m, sinks[h])
        s_scr[u % 2] = s
        m_scr[u % 2] = m

    def value_phase(u):
        b, h = units[u]
        rows = slice(h * HEAD_DIM, (h + 1) * HEAD_DIM)
        m = m_scr[u % 2]
        p = jnp.exp2(s_scr[u % 2] - m).astype(BF16)
        oe = jnp.dot(vts[b][h // group], p, preferred_element_type=F32)
        den = oe[HEAD_DIM:HEAD_DIM + 1]
        if has_sink:
            den = den + jnp.exp2(sinks[h] - m)
        ot_scr[b, rows, :] = oe[:HEAD_DIM] / den
        if emit_lse:
            lt_scr[b, rows, :] = jnp.broadcast_to(m + jnp.log2(den), (HEAD_DIM, tq))

    score_phase(0)
    for u in range(len(units)):
        if u + 1 < len(units):
            score_phase(u + 1)
        value_phase(u)
    for b in range(n_seq):
        o_ref[b] = ot_scr[b].T.astype(o_ref.dtype)
        if emit_lse:
            lse_ref[b] = lt_scr[b].T


def _band_bias(tq, radius, slopes):
    width = tq + 2 * radius
    col = lax.broadcasted_iota(jnp.int32, (width, tq), 0)
    row = lax.broadcasted_iota(jnp.int32, (width, tq), 1)
    dist = jnp.abs(col - radius - row)
    slope2 = jnp.asarray(np.asarray(slopes, np.float32) * np.float32(LOG2E))
    alibi = -slope2[:, None, None] * dist.astype(F32)[None]
    tables = []
    for cls in range(4):
        ok = dist <= radius
        if cls & 1:
            ok = ok & (col >= radius)
        if cls & 2:
            ok = ok & (col < radius + tq)
        tables.append(jnp.where(ok[None], alibi, NEG))
    return jnp.stack(tables, axis=0)


def _band_attention(src, *, q_col, k_col, v_col, q_width, kv_width, radius, slopes,
                    sink=None, emit_lse):
    nb, seq, _ = src.shape
    tq = min(BAND_TQ, seq)
    rb = tq // radius
    n_blk = seq // tq
    last = seq // radius - 1
    n_q = q_width // HEAD_DIM
    n_kv = kv_width // HEAD_DIM
    width = tq + 2 * radius

    n_seq = min(BAND_SEQS, nb)
    assert nb % n_seq == 0

    def cur(col):
        return lambda i, b: (b, i, col)

    def prev(col):
        return lambda i, b: (b, jnp.maximum(i * rb - 1, 0), col)

    def nxt(col):
        return lambda i, b: (b, jnp.minimum((i + 1) * rb, last), col)

    def edge_class(i, b):
        cls = (i == 0).astype(jnp.int32) + 2 * (i == n_blk - 1).astype(jnp.int32)
        return (cls, 0, 0, 0)

    in_specs = [pl.BlockSpec((n_seq, tq, q_width), cur(q_col))]
    for col in (k_col, v_col):
        in_specs += [
            pl.BlockSpec((n_seq, radius, kv_width), prev(col)),
            pl.BlockSpec((n_seq, tq, kv_width), cur(col)),
            pl.BlockSpec((n_seq, radius, kv_width), nxt(col)),
        ]
    in_specs.append(pl.BlockSpec((None, n_q, width, tq), edge_class))
    args = [src] * 7 + [_band_bias(tq, radius, slopes)]
    if sink is not None:
        in_specs.append(pl.BlockSpec(memory_space=pltpu.SMEM))
        args.append(sink)
    out_dtype = F32 if emit_lse else BF16
    out_specs = [pl.BlockSpec((n_seq, tq, q_width), lambda i, b: (b, i, 0))]
    out_shape = [jax.ShapeDtypeStruct((nb, seq, q_width), out_dtype)]
    if emit_lse:
        out_specs.append(pl.BlockSpec((n_seq, tq, q_width), lambda i, b: (b, i, 0)))
        out_shape.append(jax.ShapeDtypeStruct((nb, seq, q_width), F32))
    kernel = functools.partial(
        _band_kernel, n_seq=n_seq, n_q=n_q, n_kv=n_kv, has_sink=sink is not None,
        emit_lse=emit_lse)
    return pl.pallas_call(
        kernel,
        grid=(n_blk, nb // n_seq),
        in_specs=in_specs,
        out_specs=out_specs,
        out_shape=out_shape,
        scratch_shapes=[pltpu.VMEM((2, width, tq), F32), pltpu.VMEM((2, 1, tq), F32)]
        + [pltpu.VMEM((n_seq, q_width, tq), F32)] * (2 if emit_lse else 1),
        compiler_params=_params(("parallel", "parallel")),
        name="band_attention",
    )(*args)


def _mid_kernel(x_ref, oa_ref, o0_ref, l0_ref, o1_ref, l1_ref, o2_ref, l2_ref, wo_ref,
                g_ref, w1_ref, w3_ref, w2_ref, out_ref, slab_ref, *, tm):
    n_slab = B_WIDTH // LANES
    for base, src, (_, dil) in ((0, o1_ref, B_PATTERNS[1]), (n_slab, l1_ref, B_PATTERNS[1]),
                                (2 * n_slab, o2_ref, B_PATTERNS[2]),
                                (3 * n_slab, l2_ref, B_PATTERNS[2])):
        rows = tm // dil
        for d in range(dil):
            for s in range(n_slab):
                slab_ref[base + s, pl.ds(d, rows, stride=dil), :] = (
                    src[d, :, s * LANES:(s + 1) * LANES])
    mixed = []
    for s in range(n_slab):
        sl = slice(s * LANES, (s + 1) * LANES)
        l0, l1, l2 = l0_ref[:, sl], slab_ref[n_slab + s], slab_ref[3 * n_slab + s]
        mx = jnp.maximum(jnp.maximum(l0, l1), l2)
        e0, e1, e2 = jnp.exp2(l0 - mx), jnp.exp2(l1 - mx), jnp.exp2(l2 - mx)
        num = e0 * o0_ref[:, sl] + e1 * slab_ref[s] + e2 * slab_ref[2 * n_slab + s]
        mixed.append(num / (e0 + e1 + e2))
    ob = jnp.concatenate(mixed, axis=-1).astype(BF16)
    attn = (jnp.dot(oa_ref[...], wo_ref[:A_Q, :], preferred_element_type=F32)
            + jnp.dot(ob, wo_ref[A_Q:, :], preferred_element_type=F32))
    out_ref[...] = _ffn(x_ref[...] + attn, g_ref, w1_ref, w3_ref, w2_ref)


def _mid(x, oa, o0, l0, o1, l1, o2, l2, wo, g, w1, w3, w2):
    n, T, _ = x.shape
    tm = ROW_TILE
    d1, d2 = B_PATTERNS[1][1], B_PATTERNS[2][1]
    tok = lambda width: pl.BlockSpec((None, tm, width), lambda b, i: (b, i, 0))
    res = lambda dil: pl.BlockSpec((None, dil, tm // dil, B_WIDTH), lambda b, i: (b, 0, i, 0))
    return pl.pallas_call(
        functools.partial(_mid_kernel, tm=tm),
        grid=(n, T // tm),
        in_specs=[tok(D_MODEL), tok(A_Q), tok(B_WIDTH), tok(B_WIDTH), res(d1), res(d1),
                  res(d2), res(d2), _const_spec(wo.shape), _const_spec((1, D_MODEL)),
                  _const_spec(w1.shape), _const_spec(w3.shape), _const_spec(w2.shape)],
        out_specs=tok(D_MODEL),
        out_shape=jax.ShapeDtypeStruct((n, T, D_MODEL), F32),
        scratch_shapes=[pltpu.VMEM((4 * B_WIDTH // LANES, tm, LANES), F32)],
        compiler_params=_params(("parallel", "parallel")),
        name="mix_out_ffn",
    )(x, oa, o0, l0, o1, l1, o2, l2, wo, g, w1, w3, w2)


def _norm_rope(xh, gain, cos, sin):
    half = HEAD_DIM // 2
    y = xh * lax.rsqrt(jnp.mean(xh * xh, axis=0, keepdims=True) + EPS) * gain
    x0, x1 = y[:half], y[half:]
    return jnp.concatenate([x0 * cos - x1 * sin, x0 * sin + x1 * cos], axis=0)


def _proj_c_kernel(x_ref, g_ref, w_ref, gq_ref, gk_ref, cq_ref, sq_ref, ck_ref, sk_ref,
                   qt_ref, k_ref, vt_ref):
    h = _rms(x_ref[...], g_ref[...]).astype(BF16)
    pt = lax.dot_general(w_ref[...], h, (((1,), (1,)), ((), ())), preferred_element_type=F32)
    gq, gk = gq_ref[...], gk_ref[...]
    cq, sq, ck, sk = cq_ref[...], sq_ref[...], ck_ref[...], sk_ref[...]
    for hd in range(C_HEADS):
        rows = slice(hd * HEAD_DIM, (hd + 1) * HEAD_DIM)
        qt_ref[rows, :] = _norm_rope(pt[rows], gq, cq, sq).astype(BF16)
    one_feature = (lax.broadcasted_iota(jnp.int32, (K_COLS - HEAD_DIM, pt.shape[1]), 0) == 0
                   ).astype(F32)
    pieces = []
    for j in range(C_KV_HEADS):
        pieces += [_norm_rope(pt[C_Q + j * HEAD_DIM:C_Q + (j + 1) * HEAD_DIM], gk, ck, sk),
                   one_feature]
    k_nat = jnp.concatenate(pieces, axis=0).T
    for j in range(C_KV_HEADS):
        k_ref[j] = k_nat[:, j * K_COLS:(j + 1) * K_COLS].astype(BF16)
    vt_ref[...] = pt[C_Q + C_KV:].astype(BF16)


def _proj_c(x, g, wt, gq, gk, cq, sq, ck, sk):
    n, T, _ = x.shape
    tm = ROW_TILE
    half = HEAD_DIM // 2
    tab = pl.BlockSpec((half, tm), lambda b, i: (0, i))
    return pl.pallas_call(
        _proj_c_kernel,
        grid=(n, T // tm),
        in_specs=[
            pl.BlockSpec((None, tm, D_MODEL), lambda b, i: (b, i, 0)),
            _const_spec((1, D_MODEL)),
            _const_spec(wt.shape),
            _const_spec((HEAD_DIM, tm)),
            _const_spec((HEAD_DIM, tm)),
            tab, tab, tab, tab,
        ],
        out_specs=[
            pl.BlockSpec((None, C_Q, tm), lambda b, i: (b, 0, i)),
            pl.BlockSpec((None, C_KV_HEADS, tm, K_COLS), lambda b, i: (b, 0, i, 0)),
            pl.BlockSpec((None, None, C_KV, tm), lambda b, i: (b, i, 0, 0)),
        ],
        out_shape=[
            jax.ShapeDtypeStruct((n, C_Q, T), BF16),
            jax.ShapeDtypeStruct((n, C_KV_HEADS, T, K_COLS), BF16),
            jax.ShapeDtypeStruct((n, T // tm, C_KV, tm), BF16),
        ],
        compiler_params=_params(("parallel", "parallel")),
        name="proj_c",
    )(x, g, wt, gq, gk, cq, sq, ck, sk)


def _flash_kernel(bound_ref, qt_ref, k_ref, vt_ref, o_ref, sa_ref, sb_ref, qx_ref, mt_ref,
                  l_ref, acc_ref, *, tq, tk, n_chunks, unroll):
    ref_tile = REF_ROWS - HEAD_DIM
    first_row = lax.broadcasted_iota(jnp.int32, (ref_tile, tq), 0) == 0
    for hh in range(C_GROUP):
        qx_ref[hh, :HEAD_DIM, :] = qt_ref[hh * HEAD_DIM:(hh + 1) * HEAD_DIM, :]
        qx_ref[hh, REF_ROWS:, :] = jnp.zeros((K_COLS - REF_ROWS, tq), BF16)
    mt_ref[...] = jnp.zeros_like(mt_ref)
    bound = jnp.full((1, tq), bound_ref[0, 0], F32).astype(BF16).astype(F32)

    def scores(j, p_ref):
        kb = k_ref[pl.ds(pl.multiple_of(j * tk, tk), tk), :]
        for hh in range(C_GROUP):
            s = jnp.dot(kb, qx_ref[hh], preferred_element_type=F32)
            p = jnp.exp2(s)
            p_ref[hh] = p.astype(BF16)
            l_ref[hh] += jnp.sum(p, axis=0, keepdims=True)
            mt_ref[hh] = jnp.maximum(mt_ref[hh], jnp.max(s, axis=0, keepdims=True))

    def accumulate(j, p_ref):
        vb = vt_ref[j]
        for hh in range(C_GROUP):
            acc_ref[hh] += jnp.dot(vb, p_ref[hh], preferred_element_type=F32)

    bufs = (sa_ref, sb_ref)

    def stage(j, parity):
        scores(j + 1, bufs[1 - parity])
        accumulate(j, bufs[parity])

    def trip(t, carry):
        for u in range(unroll):
            stage(t * unroll + u, u % 2)
        return carry

    def one_pass(state):
        n_done, _ = state
        worst = None
        for hh in range(C_GROUP):
            ref = jnp.where(n_done == 0, bound, bound + mt_ref[hh])
            qx_ref[hh, HEAD_DIM:REF_ROWS, :] = jnp.where(first_row, -ref, 0.0).astype(BF16)
        acc_ref[...] = jnp.zeros_like(acc_ref)
        l_ref[...] = jnp.zeros_like(l_ref)
        mt_ref[...] = jnp.full_like(mt_ref, NEG)
        scores(0, bufs[0])
        lax.fori_loop(0, n_chunks // unroll - 1, trip, 0)
        for j in range(n_chunks - unroll, n_chunks - 1):
            stage(j, j % 2)
        accumulate(n_chunks - 1, bufs[(n_chunks - 1) % 2])
        for hh in range(C_GROUP):
            low = jnp.min(mt_ref[hh])
            worst = low if worst is None else jnp.minimum(worst, low)
        again = jnp.logical_and(n_done == 0, worst < -UNDERFLOW_GUARD)
        return n_done + 1, again.astype(jnp.int32)

    lax.while_loop(lambda state: state[1] != 0, one_pass, (jnp.int32(0), jnp.int32(1)))
    ot = jnp.concatenate([acc_ref[hh] / l_ref[hh] for hh in range(C_GROUP)], axis=0)
    o_ref[...] = ot.T.astype(BF16)


def _flash(qt, k, vt, bound):
    n, _, T = qt.shape
    tq, tk = FLASH_TQ, FLASH_TK
    n_chunks = T // tk
    unroll = min(FLASH_MAX_UNROLL, max(2, n_chunks // 2))
    assert unroll % 2 == 0 and n_chunks % unroll == 0
    width = C_GROUP * HEAD_DIM
    return pl.pallas_call(
        functools.partial(_flash_kernel, tq=tq, tk=tk, n_chunks=n_chunks, unroll=unroll),
        grid=(n, C_KV_HEADS, T // tq),
        in_specs=[
            pl.BlockSpec(memory_space=pltpu.SMEM),
            pl.BlockSpec((None, width, tq), lambda b, g, i: (b, g, i)),
            pl.BlockSpec((None, None, T, K_COLS), lambda b, g, i: (b, g, 0, 0)),
            pl.BlockSpec((None, n_chunks, HEAD_DIM, tk), lambda b, g, i: (b, 0, g, 0)),
        ],
        out_specs=pl.BlockSpec((None, tq, width), lambda b, g, i: (b, i, g)),
        out_shape=jax.ShapeDtypeStruct((n, T, C_Q), BF16),
        scratch_shapes=[
            pltpu.VMEM((C_GROUP, tk, tq), BF16), pltpu.VMEM((C_GROUP, tk, tq), BF16),
            pltpu.VMEM((C_GROUP, K_COLS, tq), BF16),
            pltpu.VMEM((C_GROUP, 1, tq), F32),
            pltpu.VMEM((C_GROUP, 1, tq), F32),
            pltpu.VMEM((C_GROUP, HEAD_DIM, tq), F32),
        ],
        compiler_params=_params(("parallel", "parallel", "parallel")),
        name="flash_c",
    )(bound, qt, k, vt)


def _out_kernel(x_ref, o_ref, wo_ref, g_ref, w1_ref, w3_ref, w2_ref, gf_ref, out_ref):
    x1 = x_ref[...] + jnp.dot(o_ref[...], wo_ref[...], preferred_element_type=F32)
    out_ref[...] = _rms(_ffn(x1, g_ref, w1_ref, w3_ref, w2_ref), gf_ref[...])


def _out(x, o, wo, g, w1, w3, w2, gf):
    n, T, _ = x.shape
    tm = ROW_TILE
    tok = pl.BlockSpec((None, tm, D_MODEL), lambda b, i: (b, i, 0))
    return pl.pallas_call(
        _out_kernel,
        grid=(n, T // tm),
        in_specs=[tok, tok, _const_spec(wo.shape), _const_spec((1, D_MODEL)),
                  _const_spec(w1.shape), _const_spec(w3.shape), _const_spec(w2.shape),
                  _const_spec((1, D_MODEL))],
        out_specs=tok,
        out_shape=jax.ShapeDtypeStruct((n, T, D_MODEL), F32),
        compiler_params=_params(("parallel", "parallel")),
        name="out_ffn_norm",
    )(x, o, wo, g, w1, w3, w2, gf)


def _rope_tables(T):
    n_freq = HEAD_DIM // 4
    inv = ROPE_THETA ** (-jnp.arange(n_freq, dtype=F32) / n_freq)
    rows = T // GRID_W
    row = jnp.repeat(jnp.arange(rows, dtype=F32), GRID_W)
    col = jnp.tile(jnp.arange(GRID_W, dtype=F32), rows)
    ang = jnp.concatenate([row[:, None] * inv, col[:, None] * inv], axis=-1)
    return jnp.cos(ang).T, jnp.sin(ang).T


def _prepare(norm_mix, w_in_ab, w_out_ab, sink_a, w_in_c, w_out_c, q_gain_c, k_gain_c,
             norm_ffn, ffn_w1, ffn_w3, ffn_w2, final_norm):
    col_scale = np.ones((1, PA_WIDTH + 2 * GROUP_WIDTH), np.float32)
    col_scale[:, :A_Q] = QK_SCALE * LOG2E
    for gidx in range(len(B_PATTERNS)):
        base = A_Q + 2 * A_KV + gidx * GROUP_WIDTH
        col_scale[:, base:base + B_WIDTH] = QK_SCALE * LOG2E
    pair_perm = np.concatenate([np.arange(0, HEAD_DIM, 2), np.arange(1, HEAD_DIM, 2)])
    row_perm = np.concatenate(
        [h * HEAD_DIM + pair_perm for h in range(C_HEADS + C_KV_HEADS)]
        + [np.arange(C_Q + C_KV, C_IN)])
    row = lambda v: v.reshape(1, D_MODEL).astype(F32)
    gain = lambda v: jnp.broadcast_to(v.astype(F32)[pair_perm][:, None], (HEAD_DIM, ROW_TILE))
    return dict(
        g_mix0=row(norm_mix[0]), g_mix1=row(norm_mix[1]),
        g_ffn0=row(norm_ffn[0]), g_ffn1=row(norm_ffn[1]), g_final=row(final_norm),
        w_ab=(w_in_ab[0] * col_scale).astype(BF16),
        wo_ab=w_out_ab[0].astype(BF16),
        sink=sink_a[0].reshape(1, A_HEADS).astype(F32),
        wt_c=w_in_c[0].T[row_perm].astype(BF16),
        wo_c=w_out_c[0].astype(BF16),
        gq=gain(q_gain_c[0]), gk=gain(k_gain_c[0]),
        score_bound=(1.02 * HEAD_DIM * QK_SCALE * LOG2E * jnp.max(jnp.abs(q_gain_c[0]))
                     * jnp.max(jnp.abs(k_gain_c[0]))).astype(F32).reshape(1, 1),
        w1=[ffn_w1[l].astype(BF16) for l in range(2)],
        w3=[ffn_w3[l].astype(BF16) for l in range(2)],
        w2=[ffn_w2[l].astype(BF16) for l in range(2)],
    )


def _trunk(x, p):
    n, T, _ = x.shape
    pa, pb1, pb2 = _proj_ab(x, p["g_mix0"], p["w_ab"])
    slopes_a = _alibi_slopes(A_HEADS)
    (oa,) = _band_attention(
        pa, q_col=0, k_col=A_Q // A_KV, v_col=A_Q // A_KV + 1, q_width=A_Q, kv_width=A_KV,
        radius=A_WINDOW, slopes=slopes_a, sink=p["sink"], emit_lse=False)
    slopes_b = _alibi_slopes(B_HEADS).reshape(len(B_PATTERNS), B_HEADS_PER_GROUP)
    ob, lb = [], []
    for gidx, (src, first) in enumerate((
            (pa, (A_Q + 2 * A_KV) // B_WIDTH),
            (pb1.reshape(n * B_PATTERNS[1][1], T // B_PATTERNS[1][1], GROUP_WIDTH), 0),
            (pb2.reshape(n * B_PATTERNS[2][1], T // B_PATTERNS[2][1], GROUP_WIDTH), 0))):
        dil = B_PATTERNS[gidx][1]
        o, lse = _band_attention(
            src, q_col=first, k_col=first + 1, v_col=first + 2, q_width=B_WIDTH,
            kv_width=B_WIDTH, radius=B_RADIUS, slopes=slopes_b[gidx] * np.float32(dil),
            emit_lse=True)
        ob.append(o.reshape(n, dil, T // dil, B_WIDTH) if dil > 1 else o)
        lb.append(lse.reshape(n, dil, T // dil, B_WIDTH) if dil > 1 else lse)
    x = _mid(x, oa, ob[0], lb[0], ob[1], lb[1], ob[2], lb[2], p["wo_ab"], p["g_ffn0"],
             p["w1"][0], p["w3"][0], p["w2"][0])
    cos, sin = _rope_tables(T)
    qt, k, vt = _proj_c(x, p["g_mix1"], p["wt_c"], p["gq"], p["gk"],
                        cos * (QK_SCALE * LOG2E), sin * (QK_SCALE * LOG2E), cos, sin)
    o = _flash(qt, k, vt, p["score_bound"])
    return _out(x, o, p["wo_c"], p["g_ffn1"], p["w1"][1], p["w3"][1], p["w2"][1], p["g_final"])


def kernel(x_prompt, x_sample, norm_mix, w_in_ab, w_out_ab, sink_a, w_in_c, w_out_c, q_gain_c,
           k_gain_c, norm_ffn, ffn_w1, ffn_w3, ffn_w2, final_norm):
    assert norm_mix.shape[0] == 2 and all(window // (2 * dil) == B_RADIUS
                                          for window, dil in B_PATTERNS)
    p = _prepare(norm_mix, w_in_ab, w_out_ab, sink_a, w_in_c, w_out_c, q_gain_c, k_gain_c,
                 norm_ffn, ffn_w1, ffn_w3, ffn_w2, final_norm)
    return _trunk(x_prompt, p), _trunk(x_sample, p)
```
